```python
import jax, jax.numpy as jnp
from jax import lax
import numpy as np

D_MODEL = 1024
BATCH = 8
SEQ = 2048
DEPTH = 4

GRID_W = 64
CTX_LEN = 256
N_MIXERS = 2
N_HEADS = 16
N_KV_HEADS = 4
HEAD_DIM = D_MODEL // N_HEADS
GROUP = N_HEADS // N_KV_HEADS
Q_DIM = N_HEADS * HEAD_DIM
KV_DIM = N_KV_HEADS * HEAD_DIM
WINDOW = 128
BLOCK = 128
ROPE_THETA = 10000.0
ROPE_QUARTER = HEAD_DIM // 4
CONV_WIDTH = 3
D_FF = 2816
N_MOD = 9
RMS_EPS = 1e-6
NEG_INF = -1e30
HALF_STEP = 0.5

kernel_name = 'hybrid_shortconv_swa_macaron_dit'


def rmsnorm(x, g):
    xf = x.astype(jnp.float32)
    y = xf * lax.rsqrt(jnp.mean(xf * xf, axis=-1, keepdims=True) + RMS_EPS)
    return (y * g.astype(jnp.float32)).astype(x.dtype)


def adaln_in(x, g, shift, scale):
    return rmsnorm(x, g) * (1 + scale) + shift


def adaln_out(x, y, g, gate, weight):
    return x + weight * gate * rmsnorm(y, g)


def swiglu(h, w_gu, w_down):
    gu = h @ w_gu
    return (jax.nn.silu(gu[..., :D_FF]) * gu[..., D_FF:]) @ w_down


def ffn_sublayer(x, mod, k, g_pre, g_post, w_gu, w_down):
    h = adaln_in(x, g_pre, mod[k], mod[k + 1])
    return adaln_out(x, swiglu(h, w_gu, w_down), g_post, mod[k + 2], HALF_STEP)


def short_conv_mixer(h, w_in, w_conv, w_out):
    bcu = h @ w_in
    b, cg, u = bcu[..., :D_MODEL], bcu[..., D_MODEL:2 * D_MODEL], bcu[..., 2 * D_MODEL:]
    up = jnp.pad(cg * u, ((0, 0), (1, 1), (0, 0)))
    y = w_conv[0] * up[:, :-2] + w_conv[1] * up[:, 1:-1] + w_conv[2] * up[:, 2:]
    return (b * y) @ w_out


def axial_rope_tables(n_tokens, dtype):
    rows = n_tokens // GRID_W
    row = jnp.broadcast_to(jnp.arange(rows)[:, None], (rows, GRID_W)).reshape(-1).astype(jnp.float32)
    col = jnp.broadcast_to(jnp.arange(GRID_W)[None, :], (rows, GRID_W)).reshape(-1).astype(jnp.float32)
    inv_freq = ROPE_THETA ** (-jnp.arange(ROPE_QUARTER, dtype=jnp.float32) / ROPE_QUARTER)
    ang = jnp.stack([row[:, None] * inv_freq, col[:, None] * inv_freq], axis=1)
    return jnp.cos(ang)[:, None].astype(dtype), jnp.sin(ang)[:, None].astype(dtype)


def apply_rope(t, cos, sin):
    t4 = t.reshape(t.shape[:-1] + (2, 2, ROPE_QUARTER))
    x1, x2 = t4[..., 0, :], t4[..., 1, :]
    out = jnp.stack([x1 * cos - x2 * sin, x2 * cos + x1 * sin], axis=-2)
    return out.reshape(t.shape)


def banded_blocks(t, nb):
    B = t.shape[0]
    tp = jnp.pad(t, ((0, 0), (BLOCK, BLOCK), (0, 0), (0, 0)))
    tp = tp.reshape(B, nb + 2, BLOCK, N_KV_HEADS, HEAD_DIM)
    return jnp.concatenate([tp[:, :-2], tp[:, 1:-1], tp[:, 2:]], axis=2)


def band_mask(nb, n_tokens):
    a = jnp.arange(BLOCK)[:, None]
    b = jnp.arange(3 * BLOCK)[None, :]
    off = b - a
    in_win = (off >= BLOCK - WINDOW) & (off <= BLOCK + WINDOW)
    key_pos = (jnp.arange(nb)[:, None] - 1) * BLOCK + b
    valid = (key_pos >= 0) & (key_pos < n_tokens)
    return in_win[None] & valid[:, None, :]


def window_attention_mixer(hx, hz, w_qkv, w_o, sink, cos, sin, ctx_queries):
    B, S, _ = hx.shape
    C = hz.shape[1]
    nb = S // BLOCK
    nk = 3 * BLOCK
    scale = HEAD_DIM ** -0.5
    qkv = hx @ w_qkv
    q = apply_rope(qkv[..., :Q_DIM].reshape(B, S, N_HEADS, HEAD_DIM), cos, sin) * scale
    k = apply_rope(qkv[..., Q_DIM:Q_DIM + KV_DIM].reshape(B, S, N_KV_HEADS, HEAD_DIM), cos, sin)
    v = qkv[..., Q_DIM + KV_DIM:].reshape(B, S, N_KV_HEADS, HEAD_DIM)
    kvz = hz @ w_qkv[:, Q_DIM:]
    kz = kvz[..., :KV_DIM].reshape(B, C, N_KV_HEADS, HEAD_DIM)
    vz = kvz[..., KV_DIM:].reshape(B, C, N_KV_HEADS, HEAD_DIM)
    sink_l = sink.reshape(N_KV_HEADS, GROUP).astype(jnp.float32)

    qb = q.reshape(B, nb, BLOCK, N_KV_HEADS, GROUP, HEAD_DIM)
    kb = banded_blocks(k, nb)
    vb = banded_blocks(v, nb)
    s_win = jnp.einsum('bnqhgd,bnkhd->bnhgqk', qb, kb).astype(jnp.float32)
    s_win = jnp.where(band_mask(nb, S)[None, :, None, None], s_win, NEG_INF)
    s_ctx = jnp.einsum('bnqhgd,bchd->bnhgqc', qb, kz).astype(jnp.float32)
    s_sink = jnp.broadcast_to(sink_l[None, None, :, :, None, None], s_win.shape[:-1] + (1,))
    p = jax.nn.softmax(jnp.concatenate([s_win, s_ctx, s_sink], axis=-1), axis=-1).astype(v.dtype)
    o = (jnp.einsum('bnhgqk,bnkhd->bnqhgd', p[..., :nk], vb)
         + jnp.einsum('bnhgqc,bchd->bnqhgd', p[..., nk:nk + C], vz))
    yx = o.reshape(B, S, Q_DIM) @ w_o

    yz = None
    if ctx_queries:
        qz = (hz @ w_qkv[:, :Q_DIM]).reshape(B, C, N_KV_HEADS, GROUP, HEAD_DIM) * scale
        sz = jnp.einsum('bqhgd,bkhd->bhgqk', qz, kz).astype(jnp.float32)
        sz_sink = jnp.broadcast_to(sink_l[None, :, :, None, None], sz.shape[:-1] + (1,))
        pz = jax.nn.softmax(jnp.concatenate([sz, sz_sink], axis=-1), axis=-1).astype(vz.dtype)
        oz = jnp.einsum('bhgqk,bkhd->bqhgd', pz[..., :C], vz)
        yz = oz.reshape(B, C, Q_DIM) @ w_o
    return yx, yz


def setup_inputs(seed: int = 0) -> dict:
    key = jax.random.key(seed)
    ks = jax.random.split(key, 16)
    n_conv = (DEPTH + 1) // N_MIXERS
    n_attn = DEPTH // N_MIXERS

    def w(k, shape, fan_in, gain=1.0):
        return gain * fan_in ** -0.5 * jax.random.normal(k, shape, jnp.float32)

    return {
        'x': jax.random.normal(ks[0], (BATCH, SEQ, D_MODEL), jnp.float32),
        'c': jax.random.normal(ks[1], (BATCH, D_MODEL), jnp.float32),
        'ctx': jax.random.normal(ks[2], (BATCH, CTX_LEN, D_MODEL), jnp.float32),
        'c_ctx': jax.random.normal(ks[3], (D_MODEL,), jnp.float32),
        'w_mod': w(ks[4], (DEPTH, D_MODEL, N_MOD * D_MODEL), D_MODEL, 0.5),
        'b_mod': 0.02 * jax.random.normal(ks[5], (DEPTH, N_MOD * D_MODEL), jnp.float32),
        'norm_g': 1.0 + 0.05 * jax.random.normal(ks[6], (DEPTH, 6, D_MODEL), jnp.float32),
        'ffn_w_gu': w(ks[7], (DEPTH, 2, D_MODEL, 2 * D_FF), D_MODEL),
        'ffn_w_down': w(ks[8], (DEPTH, 2, D_FF, D_MODEL), D_FF),
        'conv_w_in': w(ks[9], (n_conv, D_MODEL, 3 * D_MODEL), D_MODEL),
        'conv_w': w(ks[10], (n_conv, CONV_WIDTH, D_MODEL), CONV_WIDTH),
        'conv_w_out': w(ks[11], (n_conv, D_MODEL, D_MODEL), D_MODEL),
        'attn_w_qkv': w(ks[12], (n_attn, D_MODEL, Q_DIM + 2 * KV_DIM), D_MODEL),
        'attn_w_o': w(ks[13], (n_attn, Q_DIM, D_MODEL), Q_DIM),
        'attn_sink': 0.5 * jax.random.normal(ks[14], (n_attn, N_HEADS), jnp.float32),
    }


def reference(x, c, ctx, c_ctx, w_mod, b_mod, norm_g, ffn_w_gu, ffn_w_down,
              conv_w_in, conv_w, conv_w_out, attn_w_qkv, attn_w_o, attn_sink):
    B, S, D = x.shape
    cos, sin = axial_rope_tables(S, x.dtype)
    z = ctx
    for i in range(DEPTH):
        last = i == DEPTH - 1
        use_attn = (i % N_MIXERS) == 1
        j = i // N_MIXERS
        g = norm_g[i]
        mx = (jax.nn.silu(c) @ w_mod[i] + b_mod[i]).reshape(B, N_MOD, 1, D).transpose(1, 0, 2, 3)
        mz = (jax.nn.silu(c_ctx) @ w_mod[i] + b_mod[i]).reshape(N_MOD, D)
        ctx_needed = (not last) or use_attn

        x = ffn_sublayer(x, mx, 0, g[0], g[1], ffn_w_gu[i, 0], ffn_w_down[i, 0])
        if ctx_needed:
            z = ffn_sublayer(z, mz, 0, g[0], g[1], ffn_w_gu[i, 0], ffn_w_down[i, 0])

        hx = adaln_in(x, g[2], mx[3], mx[4])
        if use_attn:
            hz = adaln_in(z, g[2], mz[3], mz[4])
            yx, yz = window_attention_mixer(hx, hz, attn_w_qkv[j], attn_w_o[j], attn_sink[j],
                                            cos, sin, not last)
        else:
            yx = short_conv_mixer(hx, conv_w_in[j], conv_w[j], conv_w_out[j])
            yz = None
            if not last:
                hz = adaln_in(z, g[2], mz[3], mz[4])
                yz = short_conv_mixer(hz, conv_w_in[j], conv_w[j], conv_w_out[j])
        x = adaln_out(x, yx, g[3], mx[5], 1.0)

        x = ffn_sublayer(x, mx, 6, g[4], g[5], ffn_w_gu[i, 1], ffn_w_down[i, 1])
        if not last:
            z = adaln_out(z, yz, g[3], mz[5], 1.0)
            z = ffn_sublayer(z, mz, 6, g[4], g[5], ffn_w_gu[i, 1], ffn_w_down[i, 1])
    return x
```

```python
import functools

import jax
import jax.numpy as jnp
from jax import lax
from jax.experimental import pallas as pl
from jax.experimental.pallas import tpu as pltpu

D_MODEL = 1024
DEPTH = 4
GRID_W = 64
N_MIXERS = 2
N_HEADS = 16
N_KV_HEADS = 4
HEAD_DIM = D_MODEL // N_HEADS
GROUP = N_HEADS // N_KV_HEADS
Q_DIM = N_HEADS * HEAD_DIM
KV_DIM = N_KV_HEADS * HEAD_DIM
WINDOW = 128
BLOCK = 128
ROPE_THETA = 10000.0
ROPE_QUARTER = HEAD_DIM // 4
CONV_WIDTH = 3
D_FF = 2816
N_MOD = 9
RMS_EPS = 1e-6
NEG_INF = -1e30
HALF_STEP = 0.5

LANES = 128
BF16_SUBLANES = 16
VMEM_LIMIT_BYTES = 56 * 1024 * 1024

FF_CHUNK = 256
CONV_CHUNK = 256
MOD_PAD_ROWS = 16

F32 = jnp.float32
BF16 = jnp.bfloat16


def _rms(x, g):
    return x * lax.rsqrt(jnp.mean(x * x, axis=-1, keepdims=True) + RMS_EPS) * g


def _params(n_grid_dims):
    return pltpu.CompilerParams(dimension_semantics=("arbitrary",) * n_grid_dims,
                                vmem_limit_bytes=VMEM_LIMIT_BYTES)


def _resident(shape):
    zeros = (0,) * len(shape)
    return pl.BlockSpec(shape, lambda *_: zeros, pipeline_mode=pl.Buffered(1))


def _mod_body(c_ref, w_ref, b_ref, o_ref):
    c = c_ref[...]
    a = (c * jax.nn.sigmoid(c)).astype(BF16)
    o_ref[...] = jnp.dot(a, w_ref[...].astype(BF16), preferred_element_type=F32) + b_ref[...]


def _modulation(cs, w_mod, b_mod):
    depth, d, n = w_mod.shape
    tn = n // 4
    return pl.pallas_call(
        _mod_body,
        grid=(depth, n // tn),
        in_specs=[pl.BlockSpec((MOD_PAD_ROWS, d), lambda l, j: (0, 0)),
                  pl.BlockSpec((None, d, tn), lambda l, j: (l, 0, j)),
                  pl.BlockSpec((None, 1, tn), lambda l, j: (l, 0, j))],
        out_specs=pl.BlockSpec((None, MOD_PAD_ROWS, tn), lambda l, j: (l, 0, j)),
        out_shape=jax.ShapeDtypeStruct((depth, MOD_PAD_ROWS, n), F32),
        compiler_params=_params(2),
        name="modulation",
    )(cs, w_mod, b_mod.reshape(depth, 1, n))


def _ffn_body(x_ref, mod_ref, g_ref, wgu_ref, wd_ref, o_ref, h_ref, act_ref, *, k, gi):
    x = x_ref[...]
    shift, scale, gate = mod_ref[k:k + 1, :], mod_ref[k + 1:k + 2, :], mod_ref[k + 2:k + 3, :]
    h_ref[...] = (_rms(x, g_ref[gi:gi + 1, :]) * (1.0 + scale) + shift).astype(BF16)
    for c in range(D_FF // FF_CHUNK):
        r = jnp.dot(h_ref[...], wgu_ref[:, 2 * c * FF_CHUNK:2 * (c + 1) * FF_CHUNK],
                    preferred_element_type=F32)
        gt, up = r[:, :FF_CHUNK], r[:, FF_CHUNK:]
        act_ref[:, c * FF_CHUNK:(c + 1) * FF_CHUNK] = (gt * jax.nn.sigmoid(gt) * up).astype(BF16)
    y = jnp.dot(act_ref[...], wd_ref[...], preferred_element_type=F32)
    o_ref[...] = x + HALF_STEP * gate * _rms(y, g_ref[gi + 1:gi + 2, :])


def _ffn(xz, mod, g, wgu, wd, *, k, gi, n_slots, tm):
    _, s, d = xz.shape
    body = functools.partial(_ffn_body, k=k, gi=gi)
    return pl.pallas_call(
        body,
        grid=(n_slots, s // tm),
        in_specs=[pl.BlockSpec((None, tm, d), lambda b, t: (b, t, 0)),
                  pl.BlockSpec((None, N_MOD, d), lambda b, t: (b, 0, 0)),
                  _resident(g.shape), _resident(wgu.shape), _resident(wd.shape)],
        out_specs=pl.BlockSpec((None, tm, d), lambda b, t: (b, t, 0)),
        out_shape=jax.ShapeDtypeStruct((n_slots, s, d), F32),
        scratch_shapes=[pltpu.VMEM((tm, d), BF16), pltpu.VMEM((tm, D_FF), BF16)],
        compiler_params=_params(2),
        name="ffn",
    )(xz, mod, g, wgu, wd)


def _conv_body(x_ref, xp_ref, xn_ref, mod_ref, g_ref, win_ref, cw_ref, wout_ref, o_ref,
               h_ref, gy_ref, *, tm, n_latent, seq, ctx_len):
    halo = BF16_SUBLANES
    rows = tm + 2 * halo
    slot, t = pl.program_id(0), pl.program_id(1)
    shift, scale, gate = mod_ref[3:4, :], mod_ref[4:5, :], mod_ref[5:6, :]
    g_in = g_ref[2:3, :]

    def modulated(v):
        return (_rms(v, g_in) * (1.0 + scale) + shift).astype(BF16)

    x = x_ref[...]
    h_ref[0:tm, :] = modulated(x)
    h_ref[tm:tm + halo, :] = modulated(xn_ref[...])
    h_ref[tm + halo:rows, :] = modulated(xp_ref[...])

    period = jnp.where(slot == n_latent, ctx_len, seq)
    pos = (t * tm + lax.broadcasted_iota(jnp.int32, (tm, 1), 0)) & (period - 1)
    has_prev = pos != 0
    has_next = pos != period - 1

    for c in range(D_MODEL // CONV_CHUNK):
        r = jnp.dot(h_ref[...], win_ref[:, 3 * c * CONV_CHUNK:3 * (c + 1) * CONV_CHUNK],
                    preferred_element_type=F32)
        bg = r[0:tm, :CONV_CHUNK]
        cu = r[:, CONV_CHUNK:2 * CONV_CHUNK] * r[:, 2 * CONV_CHUNK:]
        prev = jnp.where(has_prev, pltpu.roll(cu, 1, 0)[0:tm, :], 0.0)
        nxt = jnp.where(has_next, pltpu.roll(cu, rows - 1, 0)[0:tm, :], 0.0)
        cs = slice(c * CONV_CHUNK, (c + 1) * CONV_CHUNK)
        y = cw_ref[0:1, cs] * prev + cw_ref[1:2, cs] * cu[0:tm, :] + cw_ref[2:3, cs] * nxt
        gy_ref[:, cs] = (bg * y).astype(BF16)
    y = jnp.dot(gy_ref[...], wout_ref[...], preferred_element_type=F32)
    o_ref[...] = x + gate * _rms(y, g_ref[3:4, :])


def _conv(xz, mod, g, win, cw, wout, *, n_slots, n_latent, ctx_len, tm):
    _, s, d = xz.shape
    halo = BF16_SUBLANES
    hb = tm // halo
    last = s // halo - 1
    body = functools.partial(_conv_body, tm=tm, n_latent=n_latent, seq=s, ctx_len=ctx_len)
    return pl.pallas_call(
        body,
        grid=(n_slots, s // tm),
        in_specs=[pl.BlockSpec((None, tm, d), lambda b, t: (b, t, 0)),
                  pl.BlockSpec((None, halo, d), lambda b, t: (b, jnp.maximum(t * hb - 1, 0), 0)),
                  pl.BlockSpec((None, halo, d), lambda b, t: (b, jnp.minimum((t + 1) * hb, last), 0)),
                  pl.BlockSpec((None, N_MOD, d), lambda b, t: (b, 0, 0)),
                  _resident(g.shape), _resident(win.shape), _resident(cw.shape),
                  _resident(wout.shape)],
        out_specs=pl.BlockSpec((None, tm, d), lambda b, t: (b, t, 0)),
        out_shape=jax.ShapeDtypeStruct((n_slots, s, d), F32),
        scratch_shapes=[pltpu.VMEM((tm + 2 * halo, d), BF16), pltpu.VMEM((tm, d), BF16)],
        compiler_params=_params(2),
        name="conv_mixer",
    )(xz, xz, xz, mod, g, win, cw, wout)


def _qkv_body(x_ref, mod_ref, g_ref, w_ref, cos_ref, sin_ref, q_ref, k_ref, v_ref, h_ref):
    shift, scale = mod_ref[3:4, :], mod_ref[4:5, :]
    h_ref[...] = (_rms(x_ref[...], g_ref[2:3, :]) * (1.0 + scale) + shift).astype(BF16)
    cos, sin = cos_ref[...], sin_ref[...]
    first_half = (lax.broadcasted_iota(jnp.int32, (1, LANES), 1) & (2 * ROPE_QUARTER - 1)) < ROPE_QUARTER

    def rope(t):
        partner = jnp.where(first_half, pltpu.roll(t, LANES - ROPE_QUARTER, 1), pltpu.roll(t, ROPE_QUARTER, 1))
        return t * cos + partner * sin

    qk = jnp.dot(h_ref[...], w_ref[:, :Q_DIM + KV_DIM], preferred_element_type=F32)
    for j in range(Q_DIM // LANES):
        q_ref[:, j * LANES:(j + 1) * LANES] = (
            rope(qk[:, j * LANES:(j + 1) * LANES]) * (HEAD_DIM ** -0.5)).astype(BF16)
    for j in range(KV_DIM // LANES):
        k_ref[:, j * LANES:(j + 1) * LANES] = rope(
            qk[:, Q_DIM + j * LANES:Q_DIM + (j + 1) * LANES]).astype(BF16)
    v_ref[...] = jnp.dot(h_ref[...], w_ref[:, Q_DIM + KV_DIM:], preferred_element_type=F32).astype(BF16)


def _qkv(xz, mod, g, w, cos, sin, *, n_latent, tm):
    n_slots, s, d = xz.shape
    tok = lambda b, t: (b, t, 0)
    tab = lambda b, t: (jnp.where(b == n_latent, 1, 0), t, 0)
    return pl.pallas_call(
        _qkv_body,
        grid=(n_slots, s // tm),
        in_specs=[pl.BlockSpec((None, tm, d), tok),
                  pl.BlockSpec((None, N_MOD, d), lambda b, t: (b, 0, 0)),
                  _resident(g.shape), _resident(w.shape),
                  pl.BlockSpec((None, tm, LANES), tab), pl.BlockSpec((None, tm, LANES), tab)],
        out_specs=[pl.BlockSpec((None, tm, Q_DIM), tok), pl.BlockSpec((None, tm, KV_DIM), tok),
                   pl.BlockSpec((None, tm, KV_DIM), tok)],
        out_shape=[jax.ShapeDtypeStruct((n_slots, s, Q_DIM), BF16),
                   jax.ShapeDtypeStruct((n_slots, s, KV_DIM), BF16),
                   jax.ShapeDtypeStruct((n_slots, s, KV_DIM), BF16)],
        scratch_shapes=[pltpu.VMEM((tm, d), BF16)],
        compiler_params=_params(2),
        name="qkv_rope",
    )(xz, mod, g, w, cos, sin)


def _attn_body(sink_ref, q_ref, kp_ref, kc_ref, kn_ref, kz_ref, vp_ref, vc_ref, vn_ref, vz_ref, o_ref,
               *, n_latent, seq, ctx_len):
    slot, n = pl.program_id(0), pl.program_id(1)
    qi = lax.broadcasted_iota(jnp.int32, (BLOCK, 1), 0)
    kj = lax.broadcasted_iota(jnp.int32, (1, 3 * BLOCK + ctx_len), 1)
    off = kj - qi
    key_pos = (n - 1) * BLOCK + kj
    in_win = ((off >= BLOCK - WINDOW) & (off <= BLOCK + WINDOW) & (key_pos >= 0) & (key_pos < seq)
              & (slot != n_latent))
    mask = in_win | (kj >= 3 * BLOCK)
    for h in range(N_KV_HEADS):
        hs = slice(h * HEAD_DIM, (h + 1) * HEAD_DIM)
        keys = jnp.concatenate([kp_ref[:, hs], kc_ref[:, hs], kn_ref[:, hs], kz_ref[:, hs]], axis=0)
        vals = jnp.concatenate([vp_ref[:, hs], vc_ref[:, hs], vn_ref[:, hs], vz_ref[:, hs]], axis=0)
        for gq in range(GROUP):
            head = h * GROUP + gq
            q = q_ref[:, head * HEAD_DIM:(head + 1) * HEAD_DIM]
            s = lax.dot_general(q, keys, (((1,), (1,)), ((), ())), preferred_element_type=F32)
            s = jnp.where(mask, s, NEG_INF)
            sink = sink_ref[head]
            m = jnp.maximum(jnp.max(s, axis=-1, keepdims=True), sink)
            e = jnp.exp(s - m)
            denom = jnp.sum(e, axis=-1, keepdims=True) + jnp.exp(sink - m)
            p = (e / denom).astype(BF16)
            o_ref[:, head * HEAD_DIM:(head + 1) * HEAD_DIM] = jnp.dot(
                p, vals, preferred_element_type=F32).astype(BF16)


def _attention(q, k, v, sink, *, n_slots, n_latent, ctx_len):
    _, s, _ = q.shape
    nb = s // BLOCK
    cpb = ctx_len // BLOCK
    assert ctx_len % BLOCK == 0 and s % BLOCK == 0
    body = functools.partial(_attn_body, n_latent=n_latent, seq=s, ctx_len=ctx_len)
    prev = lambda b, n: (b, jnp.maximum(n - 1, 0), 0)
    cur = lambda b, n: (b, n, 0)
    nxt = lambda b, n: (b, jnp.minimum(n + 1, nb - 1), 0)
    ctx = lambda b, n: (n_latent, jnp.where(b == n_latent, n // cpb, b), 0)
    kv_blk = lambda f: pl.BlockSpec((None, BLOCK, KV_DIM), f)
    ctx_blk = pl.BlockSpec((None, ctx_len, KV_DIM), ctx)
    return pl.pallas_call(
        body,
        grid=(n_slots, nb),
        in_specs=[pl.BlockSpec(memory_space=pltpu.SMEM),
                  pl.BlockSpec((None, BLOCK, Q_DIM), cur),
                  kv_blk(prev), kv_blk(cur), kv_blk(nxt), ctx_blk,
                  kv_blk(prev), kv_blk(cur), kv_blk(nxt), ctx_blk],
        out_specs=pl.BlockSpec((None, BLOCK, Q_DIM), cur),
        out_shape=jax.ShapeDtypeStruct((n_slots, s, Q_DIM), BF16),
        compiler_params=_params(2),
        name="window_attention",
    )(sink, q, k, k, k, k, v, v, v, v)


def _proj_body(a_ref, x_ref, mod_ref, g_ref, w_ref, o_ref):
    y = jnp.dot(a_ref[...], w_ref[...], preferred_element_type=F32)
    o_ref[...] = x_ref[...] + mod_ref[5:6, :] * _rms(y, g_ref[3:4, :])


def _proj_out(a, xz, mod, g, w, *, n_slots, tm):
    _, s, d = xz.shape
    tok = lambda b, t: (b, t, 0)
    return pl.pallas_call(
        _proj_body,
        grid=(n_slots, s // tm),
        in_specs=[pl.BlockSpec((None, tm, a.shape[-1]), tok), pl.BlockSpec((None, tm, d), tok),
                  pl.BlockSpec((None, N_MOD, d), lambda b, t: (b, 0, 0)),
                  _resident(g.shape), _resident(w.shape)],
        out_specs=pl.BlockSpec((None, tm, d), tok),
        out_shape=jax.ShapeDtypeStruct((n_slots, s, d), F32),
        compiler_params=_params(2),
        name="attn_out_proj",
    )(a, xz, mod, g, w)


def _rope_tables(seq):
    rows = seq // GRID_W
    row = jnp.broadcast_to(jnp.arange(rows)[:, None], (rows, GRID_W)).reshape(-1).astype(F32)
    col = jnp.broadcast_to(jnp.arange(GRID_W)[None, :], (rows, GRID_W)).reshape(-1).astype(F32)
    inv_freq = ROPE_THETA ** (-jnp.arange(ROPE_QUARTER, dtype=F32) / ROPE_QUARTER)
    ang_r, ang_c = row[:, None] * inv_freq, col[:, None] * inv_freq
    cos = jnp.concatenate([jnp.cos(ang_r)] * 2 + [jnp.cos(ang_c)] * 2, axis=1)
    sin = jnp.concatenate([-jnp.sin(ang_r), jnp.sin(ang_r), -jnp.sin(ang_c), jnp.sin(ang_c)], axis=1)
    reps = LANES // HEAD_DIM
    cos, sin = jnp.tile(cos, (1, reps)), jnp.tile(sin, (1, reps))
    return jnp.stack([cos, jnp.ones_like(cos)]), jnp.stack([sin, jnp.zeros_like(sin)])


def kernel(x, c, ctx, c_ctx, w_mod, b_mod, norm_g, ffn_w_gu, ffn_w_down, conv_w_in, conv_w, conv_w_out,
           attn_w_qkv, attn_w_o, attn_sink):
    n_latent, seq, d = x.shape
    ctx_len = ctx.shape[1]
    assert d == D_MODEL and n_latent * ctx_len == seq and seq % GRID_W == 0
    assert ctx_len & (ctx_len - 1) == 0 and seq & (seq - 1) == 0
    n_slots = n_latent + 1

    nfc = D_FF // FF_CHUNK
    wgu = ffn_w_gu.astype(BF16).reshape(DEPTH, 2, d, 2, nfc, FF_CHUNK)
    wgu = wgu.transpose(0, 1, 2, 4, 3, 5).reshape(DEPTH, 2, d, 2 * D_FF)
    wd = ffn_w_down.astype(BF16)
    ncc = d // CONV_CHUNK
    n_conv = conv_w_in.shape[0]
    win = conv_w_in.astype(BF16).reshape(n_conv, d, 3, ncc, CONV_CHUNK)
    win = win.transpose(0, 1, 3, 2, 4).reshape(n_conv, d, 3 * d)
    wout = conv_w_out.astype(BF16)
    wqkv = attn_w_qkv.astype(BF16)
    wo = attn_w_o.astype(BF16)

    cs = jnp.concatenate([c, c_ctx[None, :], jnp.zeros((MOD_PAD_ROWS - n_slots, d), F32)], axis=0)
    mod_all = _modulation(cs, w_mod, b_mod)[:, :n_slots].reshape(DEPTH, n_slots, N_MOD, d)
    cos, sin = _rope_tables(seq)

    xz = jnp.concatenate([x, ctx.reshape(1, seq, d)], axis=0)
    for i in range(DEPTH):
        last = i == DEPTH - 1
        use_attn = (i % N_MIXERS) == 1
        j = i // N_MIXERS
        mod, g = mod_all[i], norm_g[i]
        live = n_latent if last else n_slots

        xz = _ffn(xz, mod, g, wgu[i, 0], wd[i, 0], k=0, gi=0,
                  n_slots=n_slots if (use_attn or not last) else n_latent, tm=512)
        if use_attn:
            q, k, v = _qkv(xz, mod, g, wqkv[j], cos, sin, n_latent=n_latent, tm=1024)
            o = _attention(q, k, v, attn_sink[j], n_slots=live, n_latent=n_latent, ctx_len=ctx_len)
            xz = _proj_out(o, xz, mod, g, wo[j], n_slots=live, tm=1024)
        else:
            xz = _conv(xz, mod, g, win[j], conv_w[j], wout[j], n_slots=live, n_latent=n_latent,
                       ctx_len=ctx_len, tm=1024)
        xz = _ffn(xz, mod, g, wgu[i, 1], wd[i, 1], k=6, gi=4, n_slots=live, tm=512)
    return xz
```

```python
import functools

import jax
import jax.numpy as jnp
from jax import lax
from jax.experimental import pallas as pl
from jax.experimental.pallas import tpu as pltpu

D_MODEL = 1024
DEPTH = 4
GRID_W = 64
N_MIXERS = 2
N_HEADS = 16
N_KV_HEADS = 4
HEAD_DIM = D_MODEL // N_HEADS
GROUP = N_HEADS // N_KV_HEADS
Q_DIM = N_HEADS * HEAD_DIM
KV_DIM = N_KV_HEADS * HEAD_DIM
WINDOW = 128
BLOCK = 128
ROPE_THETA = 10000.0
ROPE_QUARTER = HEAD_DIM // 4
CONV_WIDTH = 3
D_FF = 2816
N_MOD = 9
RMS_EPS = 1e-6
NEG_INF = -1e30
HALF_STEP = 0.5

LANES = 128
BF16_SUBLANES = 16
VMEM_LIMIT_BYTES = 56 * 1024 * 1024

FF_CHUNK = 256
CONV_CHUNK = 256
MOD_PAD_ROWS = 16

F32 = jnp.float32
BF16 = jnp.bfloat16

assert 2 * HEAD_DIM == LANES and WINDOW == BLOCK == LANES


def _rms(x, g):
    return x * lax.rsqrt(jnp.mean(x * x, axis=-1, keepdims=True) + RMS_EPS) * g


def _params(n_grid_dims):
    return pltpu.CompilerParams(dimension_semantics=("arbitrary",) * n_grid_dims,
                                vmem_limit_bytes=VMEM_LIMIT_BYTES)


def _resident(shape):
    zeros = (0,) * len(shape)
    return pl.BlockSpec(shape, lambda *_: zeros, pipeline_mode=pl.Buffered(1))


def _mod_body(c_ref, w_ref, b_ref, o_ref):
    c = c_ref[...]
    a = (c * jax.nn.sigmoid(c)).astype(BF16)
    o_ref[...] = jnp.dot(a, w_ref[...].astype(BF16), preferred_element_type=F32) + b_ref[...]


def _modulation(cs, w_mod, b_mod):
    depth, d, n = w_mod.shape
    tn = n // 4
    return pl.pallas_call(
        _mod_body,
        grid=(depth, n // tn),
        in_specs=[pl.BlockSpec((MOD_PAD_ROWS, d), lambda l, j: (0, 0)),
                  pl.BlockSpec((None, d, tn), lambda l, j: (l, 0, j)),
                  pl.BlockSpec((None, 1, tn), lambda l, j: (l, 0, j))],
        out_specs=pl.BlockSpec((None, MOD_PAD_ROWS, tn), lambda l, j: (l, 0, j)),
        out_shape=jax.ShapeDtypeStruct((depth, MOD_PAD_ROWS, n), F32),
        compiler_params=_params(2),
        name="modulation",
    )(cs, w_mod, b_mod.reshape(depth, 1, n))


def _ffn_body(x_ref, mod_ref, g_ref, wgu_ref, wd_ref, o_ref, h_ref, act_ref, *, k, gi):
    x = x_ref[...]
    shift, scale, gate = mod_ref[k:k + 1, :], mod_ref[k + 1:k + 2, :], mod_ref[k + 2:k + 3, :]
    h_ref[...] = (_rms(x, g_ref[gi:gi + 1, :]) * (1.0 + scale) + shift).astype(BF16)
    for c in range(D_FF // FF_CHUNK):
        cs = slice(c * FF_CHUNK, (c + 1) * FF_CHUNK)
        us = slice(D_FF + c * FF_CHUNK, D_FF + (c + 1) * FF_CHUNK)
        gt = jnp.dot(h_ref[...], wgu_ref[:, cs], preferred_element_type=F32)
        up = jnp.dot(h_ref[...], wgu_ref[:, us], preferred_element_type=F32)
        act_ref[:, cs] = (gt * jax.nn.sigmoid(gt) * up).astype(BF16)
    y = jnp.dot(act_ref[...], wd_ref[...], preferred_element_type=F32)
    o_ref[...] = x + HALF_STEP * gate * _rms(y, g_ref[gi + 1:gi + 2, :])


def _ffn(xz, mod, g, wgu, wd, *, k, gi, n_slots, tm):
    _, s, d = xz.shape
    body = functools.partial(_ffn_body, k=k, gi=gi)
    return pl.pallas_call(
        body,
        grid=(n_slots, s // tm),
        in_specs=[pl.BlockSpec((None, tm, d), lambda b, t: (b, t, 0)),
                  pl.BlockSpec((None, N_MOD, d), lambda b, t: (b, 0, 0)),
                  _resident(g.shape), _resident(wgu.shape), _resident(wd.shape)],
        out_specs=pl.BlockSpec((None, tm, d), lambda b, t: (b, t, 0)),
        out_shape=jax.ShapeDtypeStruct((n_slots, s, d), F32),
        scratch_shapes=[pltpu.VMEM((tm, d), BF16), pltpu.VMEM((tm, D_FF), BF16)],
        compiler_params=_params(2),
        name="ffn",
    )(xz, mod, g, wgu, wd)


def _conv_body(x_ref, xp_ref, xn_ref, mod_ref, g_ref, win_ref, cw_ref, wout_ref, o_ref,
               h_ref, gy_ref, *, tm, n_latent, seq, ctx_len):
    halo = BF16_SUBLANES
    rows = tm + 2 * halo
    slot, t = pl.program_id(0), pl.program_id(1)
    shift, scale, gate = mod_ref[3:4, :], mod_ref[4:5, :], mod_ref[5:6, :]
    g_in = g_ref[2:3, :]

    def modulated(v):
        return (_rms(v, g_in) * (1.0 + scale) + shift).astype(BF16)

    x = x_ref[...]
    h_ref[0:tm, :] = modulated(x)
    h_ref[tm:tm + halo, :] = modulated(xn_ref[...])
    h_ref[tm + halo:rows, :] = modulated(xp_ref[...])

    period = jnp.where(slot == n_latent, ctx_len, seq)
    pos = (t * tm + lax.broadcasted_iota(jnp.int32, (tm, 1), 0)) & (period - 1)
    has_prev = pos != 0
    has_next = pos != period - 1

    for c in range(D_MODEL // CONV_CHUNK):
        cs = slice(c * CONV_CHUNK, (c + 1) * CONV_CHUNK)
        cc = slice(D_MODEL + c * CONV_CHUNK, D_MODEL + (c + 1) * CONV_CHUNK)
        uc = slice(2 * D_MODEL + c * CONV_CHUNK, 2 * D_MODEL + (c + 1) * CONV_CHUNK)
        bg = jnp.dot(h_ref[0:tm, :], win_ref[:, cs], preferred_element_type=F32)
        cu = (jnp.dot(h_ref[...], win_ref[:, cc], preferred_element_type=F32)
              * jnp.dot(h_ref[...], win_ref[:, uc], preferred_element_type=F32))
        prev = jnp.where(has_prev, pltpu.roll(cu, 1, 0)[0:tm, :], 0.0)
        nxt = jnp.where(has_next, pltpu.roll(cu, rows - 1, 0)[0:tm, :], 0.0)
        y = cw_ref[0:1, cs] * prev + cw_ref[1:2, cs] * cu[0:tm, :] + cw_ref[2:3, cs] * nxt
        gy_ref[:, cs] = (bg * y).astype(BF16)
    y = jnp.dot(gy_ref[...], wout_ref[...], preferred_element_type=F32)
    o_ref[...] = x + gate * _rms(y, g_ref[3:4, :])


def _conv(xz, mod, g, win, cw, wout, *, n_slots, n_latent, ctx_len, tm):
    _, s, d = xz.shape
    halo = BF16_SUBLANES
    hb = tm // halo
    last = s // halo - 1
    body = functools.partial(_conv_body, tm=tm, n_latent=n_latent, seq=s, ctx_len=ctx_len)
    return pl.pallas_call(
        body,
        grid=(n_slots, s // tm),
        in_specs=[pl.BlockSpec((None, tm, d), lambda b, t: (b, t, 0)),
                  pl.BlockSpec((None, halo, d), lambda b, t: (b, jnp.maximum(t * hb - 1, 0), 0)),
                  pl.BlockSpec((None, halo, d), lambda b, t: (b, jnp.minimum((t + 1) * hb, last), 0)),
                  pl.BlockSpec((None, N_MOD, d), lambda b, t: (b, 0, 0)),
                  _resident(g.shape), _resident(win.shape), _resident(cw.shape),
                  _resident(wout.shape)],
        out_specs=pl.BlockSpec((None, tm, d), lambda b, t: (b, t, 0)),
        out_shape=jax.ShapeDtypeStruct((n_slots, s, d), F32),
        scratch_shapes=[pltpu.VMEM((tm + 2 * halo, d), BF16), pltpu.VMEM((tm, d), BF16)],
        compiler_params=_params(2),
        name="conv_mixer",
    )(xz, xz, xz, mod, g, win, cw, wout)


def _qkv_body(x_ref, mod_ref, g_ref, wqk_ref, wv_ref, cos_ref, sin_ref, q_ref, kt_ref, v_ref, h_ref):
    shift, scale = mod_ref[3:4, :], mod_ref[4:5, :]
    h_ref[...] = (_rms(x_ref[...], g_ref[2:3, :]) * (1.0 + scale) + shift).astype(BF16)
    cos, sin = cos_ref[...], sin_ref[...]
    first_half = (lax.broadcasted_iota(jnp.int32, (1, LANES), 1) & (2 * ROPE_QUARTER - 1)) < ROPE_QUARTER

    def rope(t):
        partner = jnp.where(first_half, pltpu.roll(t, LANES - ROPE_QUARTER, 1), pltpu.roll(t, ROPE_QUARTER, 1))
        return t * cos + partner * sin

    qk = jnp.dot(h_ref[...], wqk_ref[...], preferred_element_type=F32)
    for j in range(Q_DIM // LANES):
        q_ref[:, j * LANES:(j + 1) * LANES] = (
            rope(qk[:, j * LANES:(j + 1) * LANES]) * (HEAD_DIM ** -0.5)).astype(BF16)
    for j in range(KV_DIM // LANES):
        kt_ref[j * LANES:(j + 1) * LANES, :] = rope(
            qk[:, Q_DIM + j * LANES:Q_DIM + (j + 1) * LANES]).T.astype(BF16)
    v_ref[...] = jnp.dot(h_ref[...], wv_ref[...], preferred_element_type=F32).astype(BF16)


def _qkv(xz, mod, g, wqk, wv2, cos, sin, *, n_latent, tm):
    n_slots, s, d = xz.shape
    tok = lambda b, t: (b, t, 0)
    tab = lambda b, t: (jnp.where(b == n_latent, 1, 0), t, 0)
    return pl.pallas_call(
        _qkv_body,
        grid=(n_slots, s // tm),
        in_specs=[pl.BlockSpec((None, tm, d), tok),
                  pl.BlockSpec((None, N_MOD, d), lambda b, t: (b, 0, 0)),
                  _resident(g.shape), _resident(wqk.shape), _resident(wv2.shape),
                  pl.BlockSpec((None, tm, LANES), tab), pl.BlockSpec((None, tm, LANES), tab)],
        out_specs=[pl.BlockSpec((None, tm, Q_DIM), tok),
                   pl.BlockSpec((None, KV_DIM, tm), lambda b, t: (b, 0, t)),
                   pl.BlockSpec((None, tm, 2 * KV_DIM), tok)],
        out_shape=[jax.ShapeDtypeStruct((n_slots, s, Q_DIM), BF16),
                   jax.ShapeDtypeStruct((n_slots, KV_DIM, s), BF16),
                   jax.ShapeDtypeStruct((n_slots, s, 2 * KV_DIM), BF16)],
        scratch_shapes=[pltpu.VMEM((tm, d), BF16)],
        compiler_params=_params(2),
        name="qkv_rope",
    )(xz, mod, g, wqk, wv2, cos, sin)


def _attn_body(sink_ref, q_ref, *refs, n_qb, window, seq):
    if window:
        ktp_ref, ktc_ref, ktn_ref, ktz_ref, vp_ref, vc_ref, vn_ref, vz_ref, o_ref = refs
        kt_all = jnp.concatenate([ktp_ref[...], ktc_ref[...], ktn_ref[...]], axis=1)
        v_all = jnp.concatenate([vp_ref[...], vc_ref[...], vn_ref[...]], axis=0)
    else:
        ktz_ref, vz_ref, _, o_ref = refs
    t = pl.program_id(1)
    row = lax.broadcasted_iota(jnp.int32, (BLOCK, BLOCK), 0)
    col = lax.broadcasted_iota(jnp.int32, (BLOCK, BLOCK), 1)
    low_lanes = lax.broadcasted_iota(jnp.int32, (1, LANES), 1) < HEAD_DIM
    for i in range(n_qb):
        rows = slice(i * BLOCK, (i + 1) * BLOCK)
        if window:
            n = t * n_qb + i
            keep_prev = (col >= row) & (n > 0)
            keep_next = (col <= row) & (n < seq // BLOCK - 1)
        for h in range(N_KV_HEADS):
            hr = slice(h * HEAD_DIM, (h + 1) * HEAD_DIM)
            hl = slice(h * LANES, (h + 1) * LANES)
            if window:
                keys = jnp.concatenate([kt_all[hr, i * BLOCK:(i + 3) * BLOCK], ktz_ref[hr, :]], axis=1)
                vals = jnp.concatenate([v_all[i * BLOCK:(i + 3) * BLOCK, hl], vz_ref[:, hl]], axis=0)
            else:
                keys, vals = ktz_ref[hr, :], vz_ref[:, hl]
            nk = keys.shape[1]
            zk = jnp.zeros_like(keys)
            rhs = jnp.concatenate([jnp.concatenate([keys, zk], axis=1),
                                   jnp.concatenate([zk, keys], axis=1)], axis=0)
            low_v = lax.broadcasted_iota(jnp.int32, vals.shape, 1) < HEAD_DIM
            zv = jnp.zeros_like(vals)
            vv = jnp.concatenate([jnp.where(low_v, vals, zv), jnp.where(low_v, zv, vals)], axis=0)
            for jj in range(GROUP // 2):
                j = (GROUP // 2) * h + jj
                s = jnp.dot(q_ref[rows, j * LANES:(j + 1) * LANES], rhs, preferred_element_type=F32)
                ps, rs = [], []
                for e in range(2):
                    sink = sink_ref[2 * j + e]
                    se = s[:, e * nk:(e + 1) * nk]
                    if window:
                        se = jnp.concatenate(
                            [jnp.where(keep_prev, se[:, :BLOCK], NEG_INF), se[:, BLOCK:2 * BLOCK],
                             jnp.where(keep_next, se[:, 2 * BLOCK:3 * BLOCK], NEG_INF), se[:, 3 * BLOCK:]],
                            axis=1)
                    m = jnp.maximum(jnp.max(se, axis=-1, keepdims=True), sink)
                    p = jnp.exp(se - m)
                    denom = jnp.sum(p, axis=-1, keepdims=True) + jnp.exp(sink - m)
                    ps.append(p.astype(BF16))
                    rs.append(1.0 / denom)
                o = jnp.dot(jnp.concatenate(ps, axis=1), vv, preferred_element_type=F32)
                o = o * jnp.where(low_lanes, rs[0], rs[1])
                o_ref[rows, j * LANES:(j + 1) * LANES] = o.astype(BF16)


def _attention_latent(q, kt, v2, sink, *, n_out_slots, n_latent, ctx_len, tq):
    _, s, _ = q.shape
    assert tq % BLOCK == 0 and s % tq == 0
    qpb = tq // BLOCK
    nb = s // BLOCK
    body = functools.partial(_attn_body, n_qb=qpb, window=True, seq=s)
    kt_spec = lambda w, f: pl.BlockSpec((None, KV_DIM, w), f)
    v_spec = lambda w, f: pl.BlockSpec((None, w, 2 * KV_DIM), f)
    return pl.pallas_call(
        body,
        grid=(n_latent, s // tq),
        in_specs=[pl.BlockSpec(memory_space=pltpu.SMEM),
                  pl.BlockSpec((None, tq, Q_DIM), lambda b, t: (b, t, 0)),
                  kt_spec(BLOCK, lambda b, t: (b, 0, jnp.maximum(t * qpb - 1, 0))),
                  kt_spec(tq, lambda b, t: (b, 0, t)),
                  kt_spec(BLOCK, lambda b, t: (b, 0, jnp.minimum((t + 1) * qpb, nb - 1))),
                  kt_spec(ctx_len, lambda b, t: (n_latent, 0, b)),
                  v_spec(BLOCK, lambda b, t: (b, jnp.maximum(t * qpb - 1, 0), 0)),
                  v_spec(tq, lambda b, t: (b, t, 0)),
                  v_spec(BLOCK, lambda b, t: (b, jnp.minimum((t + 1) * qpb, nb - 1), 0)),
                  v_spec(ctx_len, lambda b, t: (n_latent, b, 0))],
        out_specs=pl.BlockSpec((None, tq, Q_DIM), lambda b, t: (b, t, 0)),
        out_shape=jax.ShapeDtypeStruct((n_out_slots, s, Q_DIM), BF16),
        compiler_params=_params(2),
        name="window_attention",
    )(sink, q, kt, kt, kt, kt, v2, v2, v2, v2)


def _attention_context(o, q, kt, v2, sink, *, n_latent, ctx_len):
    _, s, _ = q.shape
    assert ctx_len % BLOCK == 0
    body = functools.partial(_attn_body, n_qb=ctx_len // BLOCK, window=False, seq=s)
    return pl.pallas_call(
        body,
        grid=(1, n_latent),
        in_specs=[pl.BlockSpec(memory_space=pltpu.SMEM),
                  pl.BlockSpec((None, ctx_len, Q_DIM), lambda _, b: (n_latent, b, 0)),
                  pl.BlockSpec((None, KV_DIM, ctx_len), lambda _, b: (n_latent, 0, b)),
                  pl.BlockSpec((None, ctx_len, 2 * KV_DIM), lambda _, b: (n_latent, b, 0)),
                  pl.BlockSpec(memory_space=pl.ANY)],
        out_specs=pl.BlockSpec((None, ctx_len, Q_DIM), lambda _, b: (n_latent, b, 0)),
        out_shape=jax.ShapeDtypeStruct(o.shape, o.dtype),
        input_output_aliases={4: 0},
        compiler_params=_params(2),
        name="context_attention",
    )(sink, q, kt, v2, o)


def _proj_body(a_ref, x_ref, mod_ref, g_ref, w_ref, o_ref):
    y = jnp.dot(a_ref[...], w_ref[...], preferred_element_type=F32)
    o_ref[...] = x_ref[...] + mod_ref[5:6, :] * _rms(y, g_ref[3:4, :])


def _proj_out(a, xz, mod, g, w, *, n_slots, tm):
    _, s, d = xz.shape
    tok = lambda b, t: (b, t, 0)
    return pl.pallas_call(
        _proj_body,
        grid=(n_slots, s // tm),
        in_specs=[pl.BlockSpec((None, tm, a.shape[-1]), tok), pl.BlockSpec((None, tm, d), tok),
                  pl.BlockSpec((None, N_MOD, d), lambda b, t: (b, 0, 0)),
                  _resident(g.shape), _resident(w.shape)],
        out_specs=pl.BlockSpec((None, tm, d), tok),
        out_shape=jax.ShapeDtypeStruct((n_slots, s, d), F32),
        compiler_params=_params(2),
        name="attn_out_proj",
    )(a, xz, mod, g, w)


def _rope_tables(seq):
    rows = seq // GRID_W
    row = jnp.broadcast_to(jnp.arange(rows)[:, None], (rows, GRID_W)).reshape(-1).astype(F32)
    col = jnp.broadcast_to(jnp.arange(GRID_W)[None, :], (rows, GRID_W)).reshape(-1).astype(F32)
    inv_freq = ROPE_THETA ** (-jnp.arange(ROPE_QUARTER, dtype=F32) / ROPE_QUARTER)
    ang_r, ang_c = row[:, None] * inv_freq, col[:, None] * inv_freq
    cos = jnp.concatenate([jnp.cos(ang_r)] * 2 + [jnp.cos(ang_c)] * 2, axis=1)
    sin = jnp.concatenate([-jnp.sin(ang_r), jnp.sin(ang_r), -jnp.sin(ang_c), jnp.sin(ang_c)], axis=1)
    reps = LANES // HEAD_DIM
    cos, sin = jnp.tile(cos, (1, reps)), jnp.tile(sin, (1, reps))
    return jnp.stack([cos, jnp.ones_like(cos)]), jnp.stack([sin, jnp.zeros_like(sin)])


def kernel(x, c, ctx, c_ctx, w_mod, b_mod, norm_g, ffn_w_gu, ffn_w_down, conv_w_in, conv_w, conv_w_out,
           attn_w_qkv, attn_w_o, attn_sink):
    n_latent, seq, d = x.shape
    ctx_len = ctx.shape[1]
    assert d == D_MODEL and n_latent * ctx_len == seq and seq % GRID_W == 0
    assert ctx_len & (ctx_len - 1) == 0 and seq & (seq - 1) == 0
    n_slots = n_latent + 1

    wgu = ffn_w_gu.astype(BF16)
    wd = ffn_w_down.astype(BF16)
    win = conv_w_in.astype(BF16)
    wout = conv_w_out.astype(BF16)
    wqk = attn_w_qkv[:, :, :Q_DIM + KV_DIM].astype(BF16)
    n_attn = attn_w_qkv.shape[0]
    wv2 = jnp.broadcast_to(
        attn_w_qkv[:, :, Q_DIM + KV_DIM:].astype(BF16).reshape(n_attn, d, N_KV_HEADS, 1, HEAD_DIM),
        (n_attn, d, N_KV_HEADS, 2, HEAD_DIM)).reshape(n_attn, d, 2 * KV_DIM)
    wo = attn_w_o.astype(BF16)

    cs = jnp.concatenate([c, c_ctx[None, :], jnp.zeros((MOD_PAD_ROWS - n_slots, d), F32)], axis=0)
    mod_all = _modulation(cs, w_mod, b_mod)[:, :n_slots].reshape(DEPTH, n_slots, N_MOD, d)
    cos, sin = _rope_tables(seq)

    xz = jnp.concatenate([x, ctx.reshape(1, seq, d)], axis=0)
    for i in range(DEPTH):
        last = i == DEPTH - 1
        use_attn = (i % N_MIXERS) == 1
        j = i // N_MIXERS
        mod, g = mod_all[i], norm_g[i]
        live = n_latent if last else n_slots

        xz = _ffn(xz, mod, g, wgu[i, 0], wd[i, 0], k=0, gi=0,
                  n_slots=n_slots if (use_attn or not last) else n_latent, tm=1024)
        if use_attn:
            q, kt, v2 = _qkv(xz, mod, g, wqk[j], wv2[j], cos, sin, n_latent=n_latent, tm=1024)
            o = _attention_latent(q, kt, v2, attn_sink[j], n_out_slots=live, n_latent=n_latent,
                                  ctx_len=ctx_len, tq=2 * BLOCK)
            if not last:
                o = _attention_context(o, q, kt, v2, attn_sink[j], n_latent=n_latent, ctx_len=ctx_len)
            xz = _proj_out(o, xz, mod, g, wo[j], n_slots=live, tm=1024)
        else:
            xz = _conv(xz, mod, g, win[j], conv_w[j], wout[j], n_slots=live, n_latent=n_latent,
                       ctx_len=ctx_len, tm=1024)
        xz = _ffn(xz, mod, g, wgu[i, 1], wd[i, 1], k=6, gi=4, n_slots=live, tm=1024)
    return xz
```

```python
import functools

import jax
import jax.numpy as jnp
from jax import lax
from jax.experimental import pallas as pl
from jax.experimental.pallas import tpu as pltpu

D_MODEL = 1024
DEPTH = 4
GRID_W = 64
N_MIXERS = 2
N_HEADS = 16
N_KV_HEADS = 4
HEAD_DIM = D_MODEL // N_HEADS
GROUP = N_HEADS // N_KV_HEADS
Q_DIM = N_HEADS * HEAD_DIM
KV_DIM = N_KV_HEADS * HEAD_DIM
WINDOW = 128
BLOCK = 128
ROPE_THETA = 10000.0
ROPE_QUARTER = HEAD_DIM // 4
CONV_WIDTH = 3
D_FF = 2816
N_MOD = 9
RMS_EPS = 1e-6
NEG_INF = -1e30
HALF_STEP = 0.5
LOG2_E = 1.4426950408889634

LANES = 128
BF16_SUBLANES = 16
VMEM_LIMIT_BYTES = 56 * 1024 * 1024

FF_CHUNK = 256
CONV_CHUNK = 256
MOD_PAD_ROWS = 16
FFN_ROWS = 1024
FFN_ROW_GROUPS = 4
MIX_ROWS = 1024
ATTN_ROWS = 4 * BLOCK

F32 = jnp.float32
BF16 = jnp.bfloat16

assert 2 * HEAD_DIM == LANES and WINDOW == BLOCK == LANES


def _rms(x, g):
    return x * lax.rsqrt(jnp.mean(x * x, axis=-1, keepdims=True) + RMS_EPS) * g


def _params(n_grid_dims):
    return pltpu.CompilerParams(dimension_semantics=("arbitrary",) * n_grid_dims,
                                vmem_limit_bytes=VMEM_LIMIT_BYTES)


def _resident(arr, *lead):
    rest = arr.shape[len(lead):]
    index = tuple(lead) + (0,) * len(rest)
    return pl.BlockSpec((None,) * len(lead) + rest, lambda *_: index, pipeline_mode=pl.Buffered(1))


def _mod_spec(mod_all, layer):
    return pl.BlockSpec((None, None) + mod_all.shape[2:], lambda b, t: (layer, b, 0, 0))


def _mod_body(c_ref, w_ref, b_ref, o_ref):
    c = c_ref[...]
    a = (c * jax.nn.sigmoid(c)).astype(BF16)
    o_ref[...] = jnp.dot(a, w_ref[...].astype(BF16), preferred_element_type=F32) + b_ref[...]


def _modulation(cs, w_mod, b_mod):
    depth, d, n = w_mod.shape
    tn = n // 4
    return pl.pallas_call(
        _mod_body,
        grid=(depth, n // tn),
        in_specs=[pl.BlockSpec((MOD_PAD_ROWS, d), lambda l, j: (0, 0)),
                  pl.BlockSpec((None, d, tn), lambda l, j: (l, 0, j)),
                  pl.BlockSpec((None, 1, tn), lambda l, j: (l, 0, j))],
        out_specs=pl.BlockSpec((None, MOD_PAD_ROWS, tn), lambda l, j: (l, 0, j)),
        out_shape=jax.ShapeDtypeStruct((depth, MOD_PAD_ROWS, n), F32),
        compiler_params=_params(2),
        name="modulation",
    )(cs, w_mod, b_mod.reshape(depth, 1, n))


def _ffn_body(*refs, k, gi, groups, n_latent, split_input):
    if split_input:
        x_ref, ctx_ref, mod_ref, g_ref, wgu_ref, wd_ref, o_ref, h_ref, act_ref = refs
    else:
        x_ref, mod_ref, g_ref, wgu_ref, wd_ref, o_ref, h_ref, act_ref = refs
    shift, scale, gate = mod_ref[k:k + 1, :], mod_ref[k + 1:k + 2, :], mod_ref[k + 2:k + 3, :]
    rows = x_ref.shape[0] // groups
    for r in range(groups):
        rs = slice(r * rows, (r + 1) * rows)
        x = x_ref[rs, :]
        if split_input:
            x = jnp.where(pl.program_id(0) == n_latent, ctx_ref[rs, :], x)
        h_ref[rs, :] = (_rms(x, g_ref[gi:gi + 1, :]) * (1.0 + scale) + shift).astype(BF16)
        for c in range(D_FF // FF_CHUNK):
            cs = slice(c * FF_CHUNK, (c + 1) * FF_CHUNK)
            us = slice(D_FF + c * FF_CHUNK, D_FF + (c + 1) * FF_CHUNK)
            gt = jnp.dot(h_ref[rs, :], wgu_ref[:, cs], preferred_element_type=F32)
            up = jnp.dot(h_ref[rs, :], wgu_ref[:, us], preferred_element_type=F32)
            act_ref[rs, cs] = (gt * jax.nn.sigmoid(gt) * up).astype(BF16)
        y = jnp.dot(act_ref[rs, :], wd_ref[...], preferred_element_type=F32)
        o_ref[rs, :] = x + HALF_STEP * gate * _rms(y, g_ref[gi + 1:gi + 2, :])


def _ffn(xz, ctx, mod_all, norm_g, wgu, wd, *, layer, half, n_slots, n_latent, tm, groups):
    _, s, d = xz.shape
    split = ctx is not None
    body = functools.partial(_ffn_body, k=6 * half, gi=4 * half, groups=groups, n_latent=n_latent,
                             split_input=split)
    if split:
        streams = [pl.BlockSpec((None, tm, d), lambda b, t: (jnp.minimum(b, n_latent - 1), t, 0)),
                   pl.BlockSpec((None, tm, d), lambda b, t: (0, jnp.where(b == n_latent, t, 0), 0))]
        operands = [xz, ctx]
    else:
        streams = [pl.BlockSpec((None, tm, d), lambda b, t: (b, t, 0))]
        operands = [xz]
    return pl.pallas_call(
        body,
        grid=(n_slots, s // tm),
        in_specs=streams + [_mod_spec(mod_all, layer), _resident(norm_g, layer),
                            _resident(wgu, layer, half), _resident(wd, layer, half)],
        out_specs=pl.BlockSpec((None, tm, d), lambda b, t: (b, t, 0)),
        out_shape=jax.ShapeDtypeStruct((n_slots, s, d), F32),
        scratch_shapes=[pltpu.VMEM((tm, d), BF16), pltpu.VMEM((tm, D_FF), BF16)],
        compiler_params=_params(2),
        name=f"ffn_g{groups}",
    )(*operands, mod_all, norm_g, wgu, wd)


def _conv_body(x_ref, xp_ref, xn_ref, mod_ref, g_ref, win_ref, cw_ref, wout_ref, o_ref,
               h_ref, gy_ref, *, tm, n_latent, seq, ctx_len):
    halo = BF16_SUBLANES
    rows = tm + 2 * halo
    slot, t = pl.program_id(0), pl.program_id(1)
    shift, scale, gate = mod_ref[3:4, :], mod_ref[4:5, :], mod_ref[5:6, :]
    g_in = g_ref[2:3, :]

    def modulated(v):
        return (_rms(v, g_in) * (1.0 + scale) + shift).astype(BF16)

    x = x_ref[...]
    h_ref[0:tm, :] = modulated(x)
    h_ref[tm:tm + halo, :] = modulated(xn_ref[...])
    h_ref[tm + halo:rows, :] = modulated(xp_ref[...])

    period = jnp.where(slot == n_latent, ctx_len, seq)
    pos = (t * tm + lax.broadcasted_iota(jnp.int32, (tm, 1), 0)) & (period - 1)
    has_prev = pos != 0
    has_next = pos != period - 1

    for c in range(D_MODEL // CONV_CHUNK):
        cs = slice(c * CONV_CHUNK, (c + 1) * CONV_CHUNK)
        cc = slice(D_MODEL + c * CONV_CHUNK, D_MODEL + (c + 1) * CONV_CHUNK)
        uc = slice(2 * D_MODEL + c * CONV_CHUNK, 2 * D_MODEL + (c + 1) * CONV_CHUNK)
        bg = jnp.dot(h_ref[0:tm, :], win_ref[:, cs], preferred_element_type=F32)
        cu = (jnp.dot(h_ref[...], win_ref[:, cc], preferred_element_type=F32)
              * jnp.dot(h_ref[...], win_ref[:, uc], preferred_element_type=F32))
        prev = jnp.where(has_prev, pltpu.roll(cu, 1, 0)[0:tm, :], 0.0)
        nxt = jnp.where(has_next, pltpu.roll(cu, rows - 1, 0)[0:tm, :], 0.0)
        y = cw_ref[0:1, cs] * prev + cw_ref[1:2, cs] * cu[0:tm, :] + cw_ref[2:3, cs] * nxt
        gy_ref[:, cs] = (bg * y).astype(BF16)
    y = jnp.dot(gy_ref[...], wout_ref[...], preferred_element_type=F32)
    o_ref[...] = x + gate * _rms(y, g_ref[3:4, :])


def _conv(xz, mod_all, norm_g, win, cw, wout, *, layer, j, n_slots, n_latent, ctx_len, tm):
    _, s, d = xz.shape
    halo = BF16_SUBLANES
    hb = tm // halo
    last = s // halo - 1
    body = functools.partial(_conv_body, tm=tm, n_latent=n_latent, seq=s, ctx_len=ctx_len)
    return pl.pallas_call(
        body,
        grid=(n_slots, s // tm),
        in_specs=[pl.BlockSpec((None, tm, d), lambda b, t: (b, t, 0)),
                  pl.BlockSpec((None, halo, d), lambda b, t: (b, jnp.maximum(t * hb - 1, 0), 0)),
                  pl.BlockSpec((None, halo, d), lambda b, t: (b, jnp.minimum((t + 1) * hb, last), 0)),
                  _mod_spec(mod_all, layer), _resident(norm_g, layer), _resident(win, j),
                  _resident(cw, j), _resident(wout, j)],
        out_specs=pl.BlockSpec((None, tm, d), lambda b, t: (b, t, 0)),
        out_shape=jax.ShapeDtypeStruct((n_slots, s, d), F32),
        scratch_shapes=[pltpu.VMEM((tm + 2 * halo, d), BF16), pltpu.VMEM((tm, d), BF16)],
        compiler_params=_params(2),
        name="conv_mixer",
    )(xz, xz, xz, mod_all, norm_g, win, cw, wout)


def _qkv_body(x_ref, mod_ref, g_ref, wqk_ref, wv_ref, cos_ref, sin_ref, q_ref, kt_ref, v_ref, h_ref,
              *, q_scale):
    shift, scale = mod_ref[3:4, :], mod_ref[4:5, :]
    h_ref[...] = (_rms(x_ref[...], g_ref[2:3, :]) * (1.0 + scale) + shift).astype(BF16)
    cos, sin = cos_ref[...], sin_ref[...]
    first_half = (lax.broadcasted_iota(jnp.int32, (1, LANES), 1) & (2 * ROPE_QUARTER - 1)) < ROPE_QUARTER

    def rope(t):
        partner = jnp.where(first_half, pltpu.roll(t, LANES - ROPE_QUARTER, 1), pltpu.roll(t, ROPE_QUARTER, 1))
        return t * cos + partner * sin

    qk = jnp.dot(h_ref[...], wqk_ref[...], preferred_element_type=F32)
    for j in range(Q_DIM // LANES):
        q_ref[:, j * LANES:(j + 1) * LANES] = (
            rope(qk[:, j * LANES:(j + 1) * LANES]) * q_scale).astype(BF16)
    for j in range(KV_DIM // LANES):
        kt_ref[j * LANES:(j + 1) * LANES, :] = rope(
            qk[:, Q_DIM + j * LANES:Q_DIM + (j + 1) * LANES]).T.astype(BF16)
    v_ref[...] = jnp.dot(h_ref[...], wv_ref[...], preferred_element_type=F32).astype(BF16)


def _qkv(xz, mod_all, norm_g, wqk, wv2, cos, sin, *, layer, j, n_latent, tm, q_scale):
    n_slots, s, d = xz.shape
    tok = lambda b, t: (b, t, 0)
    tab = lambda b, t: (jnp.where(b == n_latent, 1, 0), t, 0)
    return pl.pallas_call(
        functools.partial(_qkv_body, q_scale=q_scale),
        grid=(n_slots, s // tm),
        in_specs=[pl.BlockSpec((None, tm, d), tok),
                  _mod_spec(mod_all, layer), _resident(norm_g, layer), _resident(wqk, j),
                  _resident(wv2, j),
                  pl.BlockSpec((None, tm, LANES), tab), pl.BlockSpec((None, tm, LANES), tab)],
        out_specs=[pl.BlockSpec((None, tm, Q_DIM), tok),
                   pl.BlockSpec((None, KV_DIM, tm), lambda b, t: (b, 0, t)),
                   pl.BlockSpec((None, tm, 2 * KV_DIM), tok)],
        out_shape=[jax.ShapeDtypeStruct((n_slots, s, Q_DIM), BF16),
                   jax.ShapeDtypeStruct((n_slots, KV_DIM, s), BF16),
                   jax.ShapeDtypeStruct((n_slots, s, 2 * KV_DIM), BF16)],
        scratch_shapes=[pltpu.VMEM((tm, d), BF16)],
        compiler_params=_params(2),
        name="qkv_rope",
    )(xz, mod_all, norm_g, wqk, wv2, cos, sin)


def _attn_body(sink_ref, q_ref, *refs, n_qb, window, seq, sink_row, log2_scores):
    if window:
        ktp_ref, ktc_ref, ktn_ref, ktz_ref, vp_ref, vc_ref, vn_ref, vz_ref, o_ref = refs
        kt_all = jnp.concatenate([ktp_ref[...], ktc_ref[...], ktn_ref[...]], axis=1)
        v_all = jnp.concatenate([vp_ref[...], vc_ref[...], vn_ref[...]], axis=0)
    else:
        ktz_ref, vz_ref, _, o_ref = refs
    ex = jnp.exp2 if log2_scores else jnp.exp
    sink_scale = LOG2_E if log2_scores else 1.0
    t = pl.program_id(1)
    row = lax.broadcasted_iota(jnp.int32, (BLOCK, BLOCK), 0)
    col = lax.broadcasted_iota(jnp.int32, (BLOCK, BLOCK), 1)
    low_lanes = lax.broadcasted_iota(jnp.int32, (1, LANES), 1) < HEAD_DIM
    for i in range(n_qb):
        rows = slice(i * BLOCK, (i + 1) * BLOCK)
        if window:
            n = t * n_qb + i
            keep_prev = (col >= row) & (n > 0)
            keep_next = (col <= row) & (n < seq // BLOCK - 1)
        for h in range(N_KV_HEADS):
            hr = slice(h * HEAD_DIM, (h + 1) * HEAD_DIM)
            hl = slice(h * LANES, (h + 1) * LANES)
            if window:
                keys = jnp.concatenate([kt_all[hr, i * BLOCK:(i + 3) * BLOCK], ktz_ref[hr, :]], axis=1)
                vals = jnp.concatenate([v_all[i * BLOCK:(i + 3) * BLOCK, hl], vz_ref[:, hl]], axis=0)
            else:
                keys, vals = ktz_ref[hr, :], vz_ref[:, hl]
            nk = keys.shape[1]
            zk = jnp.zeros_like(keys)
            rhs = jnp.concatenate([jnp.concatenate([keys, zk], axis=1),
                                   jnp.concatenate([zk, keys], axis=1)], axis=0)
            low_v = lax.broadcasted_iota(jnp.int32, vals.shape, 1) < HEAD_DIM
            zv = jnp.zeros_like(vals)
            ones_lo = jnp.where(low_v, 1.0, 0.0).astype(BF16)
            ones_hi = jnp.where(low_v, 0.0, 1.0).astype(BF16)
            vv = jnp.concatenate([jnp.concatenate([jnp.where(low_v, vals, zv), ones_lo], axis=1),
                                  jnp.concatenate([jnp.where(low_v, zv, vals), ones_hi], axis=1)], axis=0)
            for jj in range(GROUP // 2):
                j = (GROUP // 2) * h + jj
                s = jnp.dot(q_ref[rows, j * LANES:(j + 1) * LANES], rhs, preferred_element_type=F32)
                ps, sink_terms = [], []
                for e in range(2):
                    sink = sink_ref[sink_row, 2 * j + e] * sink_scale
                    se = s[:, e * nk:(e + 1) * nk]
                    if window:
                        se = jnp.concatenate(
                            [jnp.where(keep_prev, se[:, :BLOCK], NEG_INF), se[:, BLOCK:2 * BLOCK],
                             jnp.where(keep_next, se[:, 2 * BLOCK:3 * BLOCK], NEG_INF), se[:, 3 * BLOCK:]],
                            axis=1)
                    m = jnp.maximum(jnp.max(se, axis=-1, keepdims=True), sink)
                    ps.append(ex(se - m).astype(BF16))
                    sink_terms.append(ex(sink - m))
                ov = jnp.dot(jnp.concatenate(ps, axis=1), vv, preferred_element_type=F32)
                denom = ov[:, LANES:] + jnp.where(low_lanes, sink_terms[0], sink_terms[1])
                o_ref[rows, j * LANES:(j + 1) * LANES] = (ov[:, :LANES] / denom).astype(BF16)


def _attention_latent(q, kt, v2, sink, *, j, n_out_slots, n_latent, ctx_len, tq, log2_scores):
    _, s, _ = q.shape
    assert tq % BLOCK == 0 and s % tq == 0
    qpb = tq // BLOCK
    nb = s // BLOCK
    body = functools.partial(_attn_body, n_qb=qpb, window=True, seq=s, sink_row=j, log2_scores=log2_scores)
    kt_spec = lambda w, f: pl.BlockSpec((None, KV_DIM, w), f)
    v_spec = lambda w, f: pl.BlockSpec((None, w, 2 * KV_DIM), f)
    return pl.pallas_call(
        body,
        grid=(n_latent, s // tq),
        in_specs=[pl.BlockSpec(memory_space=pltpu.SMEM),
                  pl.BlockSpec((None, tq, Q_DIM), lambda b, t: (b, t, 0)),
                  kt_spec(BLOCK, lambda b, t: (b, 0, jnp.maximum(t * qpb - 1, 0))),
                  kt_spec(tq, lambda b, t: (b, 0, t)),
                  kt_spec(BLOCK, lambda b, t: (b, 0, jnp.minimum((t + 1) * qpb, nb - 1))),
                  kt_spec(ctx_len, lambda b, t: (n_latent, 0, b)),
                  v_spec(BLOCK, lambda b, t: (b, jnp.maximum(t * qpb - 1, 0), 0)),
                  v_spec(tq, lambda b, t: (b, t, 0)),
                  v_spec(BLOCK, lambda b, t: (b, jnp.minimum((t + 1) * qpb, nb - 1), 0)),
                  v_spec(ctx_len, lambda b, t: (n_latent, b, 0))],
        out_specs=pl.BlockSpec((None, tq, Q_DIM), lambda b, t: (b, t, 0)),
        out_shape=jax.ShapeDtypeStruct((n_out_slots, s, Q_DIM), BF16),
        compiler_params=_params(2),
        name="window_attention",
    )(sink, q, kt, kt, kt, kt, v2, v2, v2, v2)


def _attention_context(o, q, kt, v2, sink, *, j, n_latent, ctx_len, log2_scores):
    _, s, _ = q.shape
    assert ctx_len % BLOCK == 0
    body = functools.partial(_attn_body, n_qb=ctx_len // BLOCK, window=False, seq=s, sink_row=j,
                             log2_scores=log2_scores)
    return pl.pallas_call(
        body,
        grid=(1, n_latent),
        in_specs=[pl.BlockSpec(memory_space=pltpu.SMEM),
                  pl.BlockSpec((None, ctx_len, Q_DIM), lambda _, b: (n_latent, b, 0)),
                  pl.BlockSpec((None, KV_DIM, ctx_len), lambda _, b: (n_latent, 0, b)),
                  pl.BlockSpec((None, ctx_len, 2 * KV_DIM), lambda _, b: (n_latent, b, 0)),
                  pl.BlockSpec(memory_space=pl.ANY)],
        out_specs=pl.BlockSpec((None, ctx_len, Q_DIM), lambda _, b: (n_latent, b, 0)),
        out_shape=jax.ShapeDtypeStruct(o.shape, o.dtype),
        input_output_aliases={4: 0},
        compiler_params=_params(2),
        name="context_attention",
    )(sink, q, kt, v2, o)


def _proj_body(a_ref, x_ref, mod_ref, g_ref, w_ref, o_ref):
    y = jnp.dot(a_ref[...], w_ref[...], preferred_element_type=F32)
    o_ref[...] = x_ref[...] + mod_ref[5:6, :] * _rms(y, g_ref[3:4, :])


def _proj_out(a, xz, mod_all, norm_g, w, *, layer, j, n_slots, tm):
    _, s, d = xz.shape
    tok = lambda b, t: (b, t, 0)
    return pl.pallas_call(
        _proj_body,
        grid=(n_slots, s // tm),
        in_specs=[pl.BlockSpec((None, tm, a.shape[-1]), tok), pl.BlockSpec((None, tm, d), tok),
                  _mod_spec(mod_all, layer), _resident(norm_g, layer), _resident(w, j)],
        out_specs=pl.BlockSpec((None, tm, d), tok),
        out_shape=jax.ShapeDtypeStruct((n_slots, s, d), F32),
        compiler_params=_params(2),
        name="attn_out_proj",
    )(a, xz, mod_all, norm_g, w)


def _rope_tables(seq):
    rows = seq // GRID_W
    row = jnp.broadcast_to(jnp.arange(rows)[:, None], (rows, GRID_W)).reshape(-1).astype(F32)
    col = jnp.broadcast_to(jnp.arange(GRID_W)[None, :], (rows, GRID_W)).reshape(-1).astype(F32)
    inv_freq = ROPE_THETA ** (-jnp.arange(ROPE_QUARTER, dtype=F32) / ROPE_QUARTER)
    ang_r, ang_c = row[:, None] * inv_freq, col[:, None] * inv_freq
    cos = jnp.concatenate([jnp.cos(ang_r)] * 2 + [jnp.cos(ang_c)] * 2, axis=1)
    sin = jnp.concatenate([-jnp.sin(ang_r), jnp.sin(ang_r), -jnp.sin(ang_c), jnp.sin(ang_c)], axis=1)
    reps = LANES // HEAD_DIM
    cos, sin = jnp.tile(cos, (1, reps)), jnp.tile(sin, (1, reps))
    return jnp.stack([cos, jnp.ones_like(cos)]), jnp.stack([sin, jnp.zeros_like(sin)])


def kernel(x, c, ctx, c_ctx, w_mod, b_mod, norm_g, ffn_w_gu, ffn_w_down, conv_w_in, conv_w, conv_w_out,
           attn_w_qkv, attn_w_o, attn_sink):
    n_latent, seq, d = x.shape
    ctx_len = ctx.shape[1]
    assert d == D_MODEL and n_latent * ctx_len == seq and seq % GRID_W == 0
    assert ctx_len & (ctx_len - 1) == 0 and seq & (seq - 1) == 0
    n_slots = n_latent + 1

    wgu = ffn_w_gu.astype(BF16)
    wd = ffn_w_down.astype(BF16)
    win = conv_w_in.astype(BF16)
    wout = conv_w_out.astype(BF16)
    wqk = attn_w_qkv[:, :, :Q_DIM + KV_DIM].astype(BF16)
    n_attn = attn_w_qkv.shape[0]
    wv2 = jnp.broadcast_to(
        attn_w_qkv[:, :, Q_DIM + KV_DIM:].astype(BF16).reshape(n_attn, d, N_KV_HEADS, 1, HEAD_DIM),
        (n_attn, d, N_KV_HEADS, 2, HEAD_DIM)).reshape(n_attn, d, 2 * KV_DIM)
    wo = attn_w_o.astype(BF16)

    cs = jnp.concatenate([c, c_ctx[None, :], jnp.zeros((MOD_PAD_ROWS - n_slots, d), F32)], axis=0)
    mod_all = _modulation(cs, w_mod, b_mod)[:, :n_slots].reshape(DEPTH, n_slots, N_MOD, d)
    cos, sin = _rope_tables(seq)

    xz, ctx_slot = x, ctx.reshape(1, seq, d)
    for i in range(DEPTH):
        last = i == DEPTH - 1
        use_attn = (i % N_MIXERS) == 1
        j = i // N_MIXERS
        live = n_latent if last else n_slots
        groups = FFN_ROW_GROUPS
        log2_scores = True
        q_scale = HEAD_DIM ** -0.5 * (LOG2_E if log2_scores else 1.0)

        xz = _ffn(xz, ctx_slot, mod_all, norm_g, wgu, wd, layer=i, half=0,
                  n_slots=n_slots if (use_attn or not last) else n_latent, n_latent=n_latent,
                  tm=FFN_ROWS, groups=groups)
        ctx_slot = None
        if use_attn:
            q, kt, v2 = _qkv(xz, mod_all, norm_g, wqk, wv2, cos, sin, layer=i, j=j, n_latent=n_latent,
                             tm=MIX_ROWS, q_scale=q_scale)
            o = _attention_latent(q, kt, v2, attn_sink, j=j, n_out_slots=live, n_latent=n_latent,
                                  ctx_len=ctx_len, tq=ATTN_ROWS, log2_scores=log2_scores)
            if not last:
                o = _attention_context(o, q, kt, v2, attn_sink, j=j, n_latent=n_latent, ctx_len=ctx_len,
                                       log2_scores=log2_scores)
            xz = _proj_out(o, xz, mod_all, norm_g, wo, layer=i, j=j, n_slots=live, tm=MIX_ROWS)
        else:
            xz = _conv(xz, mod_all, norm_g, win, conv_w, wout, layer=i, j=j, n_slots=live,
                       n_latent=n_latent, ctx_len=ctx_len, tm=MIX_ROWS)
        xz = _ffn(xz, None, mod_all, norm_g, wgu, wd, layer=i, half=1, n_slots=live, n_latent=n_latent,
                  tm=FFN_ROWS, groups=groups)
    return xz
```

```python
import functools

import jax
import jax.numpy as jnp
from jax import lax
from jax.experimental import pallas as pl
from jax.experimental.pallas import tpu as pltpu

D_MODEL = 1024
DEPTH = 4
GRID_W = 64
N_MIXERS = 2
N_HEADS = 16
N_KV_HEADS = 4
HEAD_DIM = D_MODEL // N_HEADS
GROUP = N_HEADS // N_KV_HEADS
Q_DIM = N_HEADS * HEAD_DIM
KV_DIM = N_KV_HEADS * HEAD_DIM
WINDOW = 128
BLOCK = 128
ROPE_THETA = 10000.0
ROPE_QUARTER = HEAD_DIM // 4
CONV_WIDTH = 3
D_FF = 2816
N_MOD = 9
RMS_EPS = 1e-6
NEG_INF = -1e30
HALF_STEP = 0.5
LOG2_E = 1.4426950408889634

LANES = 128
BF16_SUBLANES = 16
VMEM_LIMIT_BYTES = 56 * 1024 * 1024

FF_CHUNK = 256
CONV_CHUNK = 256
MOD_PAD_ROWS = 16
FFN_ROWS = 1024
FFN_ROW_GROUPS = 4
MIX_ROWS = 1024
ATTN_ROWS = 4 * BLOCK
CAST_STEPS = 16

F32 = jnp.float32
BF16 = jnp.bfloat16

assert 2 * HEAD_DIM == LANES and WINDOW == BLOCK == LANES


def _rms(x, g):
    return x * lax.rsqrt(jnp.mean(x * x, axis=-1, keepdims=True) + RMS_EPS) * g


def _params(n_grid_dims):
    return pltpu.CompilerParams(dimension_semantics=("arbitrary",) * n_grid_dims,
                                vmem_limit_bytes=VMEM_LIMIT_BYTES)


def _resident(arr, *lead):
    rest = arr.shape[len(lead):]
    index = tuple(lead) + (0,) * len(rest)
    return pl.BlockSpec((None,) * len(lead) + rest, lambda *_: index, pipeline_mode=pl.Buffered(1))


def _mod_spec(mod_all, layer):
    return pl.BlockSpec((None, None) + mod_all.shape[2:], lambda b, t: (layer, b, 0, 0))


def _mod_body(c_ref, w_ref, b_ref, o_ref):
    c = c_ref[...]
    a = (c * jax.nn.sigmoid(c)).astype(BF16)
    o_ref[...] = jnp.dot(a, w_ref[...].astype(BF16), preferred_element_type=F32) + b_ref[...]


def _modulation(cs, w_mod, b_mod):
    depth, d, n = w_mod.shape
    tn = n // 4
    return pl.pallas_call(
        _mod_body,
        grid=(depth, n // tn),
        in_specs=[pl.BlockSpec((MOD_PAD_ROWS, d), lambda l, j: (0, 0)),
                  pl.BlockSpec((None, d, tn), lambda l, j: (l, 0, j)),
                  pl.BlockSpec((None, 1, tn), lambda l, j: (l, 0, j))],
        out_specs=pl.BlockSpec((None, MOD_PAD_ROWS, tn), lambda l, j: (l, 0, j)),
        out_shape=jax.ShapeDtypeStruct((depth, MOD_PAD_ROWS, n), F32),
        compiler_params=_params(2),
        name="modulation",
    )(cs, w_mod, b_mod.reshape(depth, 1, n))


def _ffn_body(*refs, k, gi, groups, n_latent, split_input, n_casts):
    refs = list(refs)
    x_ref = refs.pop(0)
    ctx_ref = refs.pop(0) if split_input else None
    mod_ref, g_ref, wgu_ref, wd_ref = refs[:4]
    cast_src, refs = refs[4:4 + n_casts], refs[4 + n_casts:]
    o_ref = refs.pop(0)
    cast_dst, (h_ref, act_ref) = refs[:n_casts], refs[n_casts:]
    for src, dst in zip(cast_src, cast_dst):
        dst[...] = src[...].astype(BF16)
    shift, scale, gate = mod_ref[k:k + 1, :], mod_ref[k + 1:k + 2, :], mod_ref[k + 2:k + 3, :]
    rows = x_ref.shape[0] // groups
    for r in range(groups):
        rs = slice(r * rows, (r + 1) * rows)
        x = x_ref[rs, :]
        if split_input:
            x = jnp.where(pl.program_id(0) == n_latent, ctx_ref[rs, :], x)
        h_ref[rs, :] = (_rms(x, g_ref[gi:gi + 1, :]) * (1.0 + scale) + shift).astype(BF16)
        for c in range(D_FF // FF_CHUNK):
            cs = slice(c * FF_CHUNK, (c + 1) * FF_CHUNK)
            us = slice(D_FF + c * FF_CHUNK, D_FF + (c + 1) * FF_CHUNK)
            gt = jnp.dot(h_ref[rs, :], wgu_ref[:, cs], preferred_element_type=F32)
            up = jnp.dot(h_ref[rs, :], wgu_ref[:, us], preferred_element_type=F32)
            act_ref[rs, cs] = (gt * jax.nn.sigmoid(gt) * up).astype(BF16)
        y = jnp.dot(act_ref[rs, :], wd_ref[...], preferred_element_type=F32)
        o_ref[rs, :] = x + HALF_STEP * gate * _rms(y, g_ref[gi + 1:gi + 2, :])


def _ffn(xz, ctx, mod_all, norm_g, wgu, wd, casts, *, layer, half, n_slots, n_latent, tm, groups):
    _, s, d = xz.shape
    split = ctx is not None
    steps_per_slot = s // tm
    assert n_slots * steps_per_slot >= CAST_STEPS
    body = functools.partial(_ffn_body, k=6 * half, gi=4 * half, groups=groups, n_latent=n_latent,
                             split_input=split, n_casts=len(casts))
    if split:
        streams = [pl.BlockSpec((None, tm, d), lambda b, t: (jnp.minimum(b, n_latent - 1), t, 0)),
                   pl.BlockSpec((None, tm, d), lambda b, t: (0, jnp.where(b == n_latent, t, 0), 0),
                                pipeline_mode=pl.Buffered(1))]
        operands = [xz, ctx]
    else:
        streams = [pl.BlockSpec((None, tm, d), lambda b, t: (b, t, 0))]
        operands = [xz]

    def cast_block(b, t):
        return jnp.minimum(b * steps_per_slot + t, CAST_STEPS - 1)

    cast_in, cast_out, cast_shapes = [], [], []
    for src, lead in casts:
        r, c = src.shape[len(lead):]
        rows = r // CAST_STEPS
        assert r % CAST_STEPS == 0 and rows % BF16_SUBLANES == 0
        cast_in.append(pl.BlockSpec((None,) * len(lead) + (rows, c),
                                    lambda b, t, lead=tuple(lead): lead + (cast_block(b, t), 0)))
        cast_out.append(pl.BlockSpec((rows, c), lambda b, t: (cast_block(b, t), 0)))
        cast_shapes.append(jax.ShapeDtypeStruct((r, c), BF16))
    outs = pl.pallas_call(
        body,
        grid=(n_slots, steps_per_slot),
        in_specs=streams + [_mod_spec(mod_all, layer), _resident(norm_g, layer), _resident(wgu),
                            _resident(wd)] + cast_in,
        out_specs=[pl.BlockSpec((None, tm, d), lambda b, t: (b, t, 0))] + cast_out,
        out_shape=[jax.ShapeDtypeStruct((n_slots, s, d), F32)] + cast_shapes,
        scratch_shapes=[pltpu.VMEM((tm, d), BF16), pltpu.VMEM((tm, D_FF), BF16)],
        compiler_params=_params(2),
        name="ffn",
    )(*operands, mod_all, norm_g, wgu, wd, *[src for src, _ in casts])
    return outs[0], outs[1:]


def _conv_body(x_ref, xp_ref, xn_ref, mod_ref, g_ref, win_ref, cw_ref, wout_ref, o_ref,
               h_ref, gy_ref, *, tm, n_latent, seq, ctx_len):
    halo = BF16_SUBLANES
    rows = tm + 2 * halo
    slot, t = pl.program_id(0), pl.program_id(1)
    shift, scale, gate = mod_ref[3:4, :], mod_ref[4:5, :], mod_ref[5:6, :]
    g_in = g_ref[2:3, :]

    def modulated(v):
        return (_rms(v, g_in) * (1.0 + scale) + shift).astype(BF16)

    x = x_ref[...]
    h_ref[0:tm, :] = modulated(x)
    h_ref[tm:tm + halo, :] = modulated(xn_ref[...])
    h_ref[tm + halo:rows, :] = modulated(xp_ref[...])

    period = jnp.where(slot == n_latent, ctx_len, seq)
    pos = (t * tm + lax.broadcasted_iota(jnp.int32, (tm, 1), 0)) & (period - 1)
    has_prev = pos != 0
    has_next = pos != period - 1

    for c in range(D_MODEL // CONV_CHUNK):
        cs = slice(c * CONV_CHUNK, (c + 1) * CONV_CHUNK)
        cc = slice(D_MODEL + c * CONV_CHUNK, D_MODEL + (c + 1) * CONV_CHUNK)
        uc = slice(2 * D_MODEL + c * CONV_CHUNK, 2 * D_MODEL + (c + 1) * CONV_CHUNK)
        bg = jnp.dot(h_ref[0:tm, :], win_ref[:, cs], preferred_element_type=F32)
        cu = (jnp.dot(h_ref[...], win_ref[:, cc], preferred_element_type=F32)
              * jnp.dot(h_ref[...], win_ref[:, uc], preferred_element_type=F32))
        prev = jnp.where(has_prev, pltpu.roll(cu, 1, 0)[0:tm, :], 0.0)
        nxt = jnp.where(has_next, pltpu.roll(cu, rows - 1, 0)[0:tm, :], 0.0)
        y = cw_ref[0:1, cs] * prev + cw_ref[1:2, cs] * cu[0:tm, :] + cw_ref[2:3, cs] * nxt
        gy_ref[:, cs] = (bg * y).astype(BF16)
    y = jnp.dot(gy_ref[...], wout_ref[...], preferred_element_type=F32)
    o_ref[...] = x + gate * _rms(y, g_ref[3:4, :])


def _conv(xz, mod_all, norm_g, win, cw, wout, *, layer, j, n_slots, n_latent, ctx_len, tm):
    _, s, d = xz.shape
    halo = BF16_SUBLANES
    hb = tm // halo
    last = s // halo - 1
    body = functools.partial(_conv_body, tm=tm, n_latent=n_latent, seq=s, ctx_len=ctx_len)
    return pl.pallas_call(
        body,
        grid=(n_slots, s // tm),
        in_specs=[pl.BlockSpec((None, tm, d), lambda b, t: (b, t, 0)),
                  pl.BlockSpec((None, halo, d), lambda b, t: (b, jnp.maximum(t * hb - 1, 0), 0)),
                  pl.BlockSpec((None, halo, d), lambda b, t: (b, jnp.minimum((t + 1) * hb, last), 0)),
                  _mod_spec(mod_all, layer), _resident(norm_g, layer), _resident(win),
                  _resident(cw, j), _resident(wout)],
        out_specs=pl.BlockSpec((None, tm, d), lambda b, t: (b, t, 0)),
        out_shape=jax.ShapeDtypeStruct((n_slots, s, d), F32),
        scratch_shapes=[pltpu.VMEM((tm + 2 * halo, d), BF16), pltpu.VMEM((tm, d), BF16)],
        compiler_params=_params(2),
        name="conv_mixer",
    )(xz, xz, xz, mod_all, norm_g, win, cw, wout)


def _qkv_body(x_ref, mod_ref, g_ref, w_ref, cos_ref, sin_ref, q_ref, kt_ref, v_ref, h_ref, *, q_scale):
    shift, scale = mod_ref[3:4, :], mod_ref[4:5, :]
    h_ref[...] = (_rms(x_ref[...], g_ref[2:3, :]) * (1.0 + scale) + shift).astype(BF16)
    cos, sin = cos_ref[...], sin_ref[...]
    first_half = (lax.broadcasted_iota(jnp.int32, (1, LANES), 1) & (2 * ROPE_QUARTER - 1)) < ROPE_QUARTER

    def rope(t):
        partner = jnp.where(first_half, pltpu.roll(t, LANES - ROPE_QUARTER, 1), pltpu.roll(t, ROPE_QUARTER, 1))
        return t * cos + partner * sin

    qk = jnp.dot(h_ref[...], w_ref[:, :Q_DIM + KV_DIM], preferred_element_type=F32)
    for j in range(Q_DIM // LANES):
        q_ref[:, j * LANES:(j + 1) * LANES] = (
            rope(qk[:, j * LANES:(j + 1) * LANES]) * q_scale).astype(BF16)
    for j in range(KV_DIM // LANES):
        kt_ref[j * LANES:(j + 1) * LANES, :] = rope(
            qk[:, Q_DIM + j * LANES:Q_DIM + (j + 1) * LANES]).T.astype(BF16)
    v = jnp.dot(h_ref[...], w_ref[:, Q_DIM + KV_DIM:], preferred_element_type=F32)
    heads = [v[:, h * HEAD_DIM:(h + 1) * HEAD_DIM] for h in range(N_KV_HEADS)]
    v_ref[...] = jnp.concatenate([vh for vh in heads for _ in range(2)], axis=1).astype(BF16)


def _qkv(xz, mod_all, norm_g, w, cos, sin, *, layer, n_latent, tm, q_scale):
    n_slots, s, d = xz.shape
    tok = lambda b, t: (b, t, 0)
    tab = lambda b, t: (jnp.where(b == n_latent, 1, 0), t, 0)
    return pl.pallas_call(
        functools.partial(_qkv_body, q_scale=q_scale),
        grid=(n_slots, s // tm),
        in_specs=[pl.BlockSpec((None, tm, d), tok),
                  _mod_spec(mod_all, layer), _resident(norm_g, layer), _resident(w),
                  pl.BlockSpec((None, tm, LANES), tab), pl.BlockSpec((None, tm, LANES), tab)],
        out_specs=[pl.BlockSpec((None, tm, Q_DIM), tok),
                   pl.BlockSpec((None, KV_DIM, tm), lambda b, t: (b, 0, t)),
                   pl.BlockSpec((None, tm, 2 * KV_DIM), tok)],
        out_shape=[jax.ShapeDtypeStruct((n_slots, s, Q_DIM), BF16),
                   jax.ShapeDtypeStruct((n_slots, KV_DIM, s), BF16),
                   jax.ShapeDtypeStruct((n_slots, s, 2 * KV_DIM), BF16)],
        scratch_shapes=[pltpu.VMEM((tm, d), BF16)],
        compiler_params=_params(2),
        name="qkv_rope",
    )(xz, mod_all, norm_g, w, cos, sin)


def _attn_body(sink_ref, q_ref, *refs, n_qb, window, seq, sink_row):
    if window:
        ktp_ref, ktc_ref, ktn_ref, ktz_ref, vp_ref, vc_ref, vn_ref, vz_ref, o_ref = refs
        kt_all = jnp.concatenate([ktp_ref[...], ktc_ref[...], ktn_ref[...]], axis=1)
        v_all = jnp.concatenate([vp_ref[...], vc_ref[...], vn_ref[...]], axis=0)
    else:
        ktz_ref, vz_ref, _, o_ref = refs
    t = pl.program_id(1)
    row = lax.broadcasted_iota(jnp.int32, (BLOCK, BLOCK), 0)
    col = lax.broadcasted_iota(jnp.int32, (BLOCK, BLOCK), 1)
    low_lanes = lax.broadcasted_iota(jnp.int32, (1, LANES), 1) < HEAD_DIM
    for i in range(n_qb):
        rows = slice(i * BLOCK, (i + 1) * BLOCK)
        if window:
            n = t * n_qb + i
            keep_prev = (col >= row) & (n > 0)
            keep_next = (col <= row) & (n < seq // BLOCK - 1)
        for h in range(N_KV_HEADS):
            hr = slice(h * HEAD_DIM, (h + 1) * HEAD_DIM)
            hl = slice(h * LANES, (h + 1) * LANES)
            if window:
                keys = jnp.concatenate([kt_all[hr, i * BLOCK:(i + 3) * BLOCK], ktz_ref[hr, :]], axis=1)
                vals = jnp.concatenate([v_all[i * BLOCK:(i + 3) * BLOCK, hl], vz_ref[:, hl]], axis=0)
            else:
                keys, vals = ktz_ref[hr, :], vz_ref[:, hl]
            nk = keys.shape[1]
            zk = jnp.zeros_like(keys)
            rhs = jnp.concatenate([jnp.concatenate([keys, zk], axis=1),
                                   jnp.concatenate([zk, keys], axis=1)], axis=0)
            low_v = lax.broadcasted_iota(jnp.int32, vals.shape, 1) < HEAD_DIM
            zv = jnp.zeros_like(vals)
            ones_lo = jnp.where(low_v, 1.0, 0.0).astype(BF16)
            ones_hi = jnp.where(low_v, 0.0, 1.0).astype(BF16)
            vv = jnp.concatenate([jnp.concatenate([jnp.where(low_v, vals, zv), ones_lo], axis=1),
                                  jnp.concatenate([jnp.where(low_v, zv, vals), ones_hi], axis=1)], axis=0)
            for jj in range(GROUP // 2):
                j = (GROUP // 2) * h + jj
                s = jnp.dot(q_ref[rows, j * LANES:(j + 1) * LANES], rhs, preferred_element_type=F32)
                ps, sink_terms = [], []
                for e in range(2):
                    sink = sink_ref[sink_row, 2 * j + e] * LOG2_E
                    se = s[:, e * nk:(e + 1) * nk]
                    if window:
                        se = jnp.concatenate(
                            [jnp.where(keep_prev, se[:, :BLOCK], NEG_INF), se[:, BLOCK:2 * BLOCK],
                             jnp.where(keep_next, se[:, 2 * BLOCK:3 * BLOCK], NEG_INF), se[:, 3 * BLOCK:]],
                            axis=1)
                    m = jnp.maximum(jnp.max(se, axis=-1, keepdims=True), sink)
                    ps.append(jnp.exp2(se - m).astype(BF16))
                    sink_terms.append(jnp.exp2(sink - m))
                ov = jnp.dot(jnp.concatenate(ps, axis=1), vv, preferred_element_type=F32)
                denom = ov[:, LANES:] + jnp.where(low_lanes, sink_terms[0], sink_terms[1])
                o_ref[rows, j * LANES:(j + 1) * LANES] = (ov[:, :LANES] / denom).astype(BF16)


def _attention_latent(q, kt, v2, sink, *, j, n_out_slots, n_latent, ctx_len, tq):
    _, s, _ = q.shape
    assert tq % BLOCK == 0 and s % tq == 0
    qpb = tq // BLOCK
    nb = s // BLOCK
    body = functools.partial(_attn_body, n_qb=qpb, window=True, seq=s, sink_row=j)
    kt_spec = lambda w, f: pl.BlockSpec((None, KV_DIM, w), f)
    v_spec = lambda w, f: pl.BlockSpec((None, w, 2 * KV_DIM), f)
    return pl.pallas_call(
        body,
        grid=(n_latent, s // tq),
        in_specs=[pl.BlockSpec(memory_space=pltpu.SMEM),
                  pl.BlockSpec((None, tq, Q_DIM), lambda b, t: (b, t, 0)),
                  kt_spec(BLOCK, lambda b, t: (b, 0, jnp.maximum(t * qpb - 1, 0))),
                  kt_spec(tq, lambda b, t: (b, 0, t)),
                  kt_spec(BLOCK, lambda b, t: (b, 0, jnp.minimum((t + 1) * qpb, nb - 1))),
                  kt_spec(ctx_len, lambda b, t: (n_latent, 0, b)),
                  v_spec(BLOCK, lambda b, t: (b, jnp.maximum(t * qpb - 1, 0), 0)),
                  v_spec(tq, lambda b, t: (b, t, 0)),
                  v_spec(BLOCK, lambda b, t: (b, jnp.minimum((t + 1) * qpb, nb - 1), 0)),
                  v_spec(ctx_len, lambda b, t: (n_latent, b, 0))],
        out_specs=pl.BlockSpec((None, tq, Q_DIM), lambda b, t: (b, t, 0)),
        out_shape=jax.ShapeDtypeStruct((n_out_slots, s, Q_DIM), BF16),
        compiler_params=_params(2),
        name="window_attention",
    )(sink, q, kt, kt, kt, kt, v2, v2, v2, v2)


def _attention_context(o, q, kt, v2, sink, *, j, n_latent, ctx_len):
    _, s, _ = q.shape
    assert ctx_len % BLOCK == 0
    body = functools.partial(_attn_body, n_qb=ctx_len // BLOCK, window=False, seq=s, sink_row=j)
    return pl.pallas_call(
        body,
        grid=(1, n_latent),
        in_specs=[pl.BlockSpec(memory_space=pltpu.SMEM),
                  pl.BlockSpec((None, ctx_len, Q_DIM), lambda _, b: (n_latent, b, 0)),
                  pl.BlockSpec((None, KV_DIM, ctx_len), lambda _, b: (n_latent, 0, b)),
                  pl.BlockSpec((None, ctx_len, 2 * KV_DIM), lambda _, b: (n_latent, b, 0)),
                  pl.BlockSpec(memory_space=pl.ANY)],
        out_specs=pl.BlockSpec((None, ctx_len, Q_DIM), lambda _, b: (n_latent, b, 0)),
        out_shape=jax.ShapeDtypeStruct(o.shape, o.dtype),
        input_output_aliases={4: 0},
        compiler_params=_params(2),
        name="context_attention",
    )(sink, q, kt, v2, o)


def _proj_body(a_ref, x_ref, mod_ref, g_ref, w_ref, o_ref):
    y = jnp.dot(a_ref[...], w_ref[...], preferred_element_type=F32)
    o_ref[...] = x_ref[...] + mod_ref[5:6, :] * _rms(y, g_ref[3:4, :])


def _proj_out(a, xz, mod_all, norm_g, w, *, layer, n_slots, tm):
    _, s, d = xz.shape
    tok = lambda b, t: (b, t, 0)
    return pl.pallas_call(
        _proj_body,
        grid=(n_slots, s // tm),
        in_specs=[pl.BlockSpec((None, tm, a.shape[-1]), tok), pl.BlockSpec((None, tm, d), tok),
                  _mod_spec(mod_all, layer), _resident(norm_g, layer), _resident(w)],
        out_specs=pl.BlockSpec((None, tm, d), tok),
        out_shape=jax.ShapeDtypeStruct((n_slots, s, d), F32),
        compiler_params=_params(2),
        name="attn_out_proj",
    )(a, xz, mod_all, norm_g, w)


def _rope_tables(seq):
    rows = seq // GRID_W
    row = jnp.broadcast_to(jnp.arange(rows)[:, None], (rows, GRID_W)).reshape(-1).astype(F32)
    col = jnp.broadcast_to(jnp.arange(GRID_W)[None, :], (rows, GRID_W)).reshape(-1).astype(F32)
    inv_freq = ROPE_THETA ** (-jnp.arange(ROPE_QUARTER, dtype=F32) / ROPE_QUARTER)
    ang_r, ang_c = row[:, None] * inv_freq, col[:, None] * inv_freq
    cos = jnp.concatenate([jnp.cos(ang_r)] * 2 + [jnp.cos(ang_c)] * 2, axis=1)
    sin = jnp.concatenate([-jnp.sin(ang_r), jnp.sin(ang_r), -jnp.sin(ang_c), jnp.sin(ang_c)], axis=1)
    reps = LANES // HEAD_DIM
    cos, sin = jnp.tile(cos, (1, reps)), jnp.tile(sin, (1, reps))
    return jnp.stack([cos, jnp.ones_like(cos)]), jnp.stack([sin, jnp.zeros_like(sin)])


def kernel(x, c, ctx, c_ctx, w_mod, b_mod, norm_g, ffn_w_gu, ffn_w_down, conv_w_in, conv_w, conv_w_out,
           attn_w_qkv, attn_w_o, attn_sink):
    n_latent, seq, d = x.shape
    ctx_len = ctx.shape[1]
    assert d == D_MODEL and n_latent * ctx_len == seq and seq % GRID_W == 0
    assert ctx_len & (ctx_len - 1) == 0 and seq & (seq - 1) == 0
    n_slots = n_latent + 1

    cs = jnp.concatenate([c, c_ctx[None, :], jnp.zeros((MOD_PAD_ROWS - n_slots, d), F32)], axis=0)
    mod_all = _modulation(cs, w_mod, b_mod)[:, :n_slots].reshape(DEPTH, n_slots, N_MOD, d)
    cos, sin = _rope_tables(seq)

    wgu, wd = ffn_w_gu[0, 0].astype(BF16), ffn_w_down[0, 0].astype(BF16)
    xz, ctx_slot = x, ctx.reshape(1, seq, d)
    for i in range(DEPTH):
        last = i == DEPTH - 1
        use_attn = (i % N_MIXERS) == 1
        j = i // N_MIXERS
        live = n_latent if last else n_slots

        mixer_w = ([(attn_w_qkv, (j,)), (attn_w_o, (j,))] if use_attn
                   else [(conv_w_in, (j,)), (conv_w_out, (j,))])
        if i == 0:
            mix_in, mix_out = (w[lead].astype(BF16) for w, lead in mixer_w)
            mixer_w = []
        xz, cast_w = _ffn(
            xz, ctx_slot, mod_all, norm_g, wgu, wd, [(ffn_w_gu, (i, 1)), (ffn_w_down, (i, 1))] + mixer_w,
            layer=i, half=0, n_slots=n_slots if (use_attn or not last) else n_latent, n_latent=n_latent,
            tm=FFN_ROWS, groups=FFN_ROW_GROUPS)
        wgu, wd = cast_w[:2]
        if mixer_w:
            mix_in, mix_out = cast_w[2:]
        ctx_slot = None
        if use_attn:
            q, kt, v2 = _qkv(xz, mod_all, norm_g, mix_in, cos, sin, layer=i, n_latent=n_latent, tm=MIX_ROWS,
                             q_scale=HEAD_DIM ** -0.5 * LOG2_E)
            o = _attention_latent(q, kt, v2, attn_sink, j=j, n_out_slots=live, n_latent=n_latent,
                                  ctx_len=ctx_len, tq=ATTN_ROWS)
            if not last:
                o = _attention_context(o, q, kt, v2, attn_sink, j=j, n_latent=n_latent, ctx_len=ctx_len)
            xz = _proj_out(o, xz, mod_all, norm_g, mix_out, layer=i, n_slots=live, tm=MIX_ROWS)
        else:
            xz = _conv(xz, mod_all, norm_g, mix_in, conv_w, mix_out, layer=i, j=j, n_slots=live,
                       n_latent=n_latent, ctx_len=ctx_len, tm=MIX_ROWS)
        next_ffn = [] if last else [(ffn_w_gu, (i + 1, 0)), (ffn_w_down, (i + 1, 0))]
        xz, next_w = _ffn(xz, None, mod_all, norm_g, wgu, wd, next_ffn, layer=i, half=1, n_slots=live,
                          n_latent=n_latent, tm=FFN_ROWS, groups=FFN_ROW_GROUPS)
        if not last:
            wgu, wd = next_w
    return xz
```

```python
import functools

import jax
import jax.numpy as jnp
from jax import lax
from jax.experimental import pallas as pl
from jax.experimental.pallas import tpu as pltpu

D_MODEL = 1024
DEPTH = 4
GRID_W = 64
N_MIXERS = 2
N_HEADS = 16
N_KV_HEADS = 4
HEAD_DIM = D_MODEL // N_HEADS
GROUP = N_HEADS // N_KV_HEADS
Q_DIM = N_HEADS * HEAD_DIM
KV_DIM = N_KV_HEADS * HEAD_DIM
WINDOW = 128
BLOCK = 128
ROPE_THETA = 10000.0
ROPE_QUARTER = HEAD_DIM // 4
CONV_WIDTH = 3
D_FF = 2816
N_MOD = 9
RMS_EPS = 1e-6
NEG_INF = -1e30
HALF_STEP = 0.5
LOG2_E = 1.4426950408889634

LANES = 128
BF16_SUBLANES = 16
VMEM_LIMIT_BYTES = 56 * 1024 * 1024

FF_CHUNK = 256
CONV_CHUNK = 256
MOD_PAD_ROWS = 16
FFN_ROWS = 1024
FFN_ROW_GROUPS = 4
MIX_ROWS = 1024
ATTN_ROWS = 4 * BLOCK
CAST_STEPS = 16

F32 = jnp.float32
BF16 = jnp.bfloat16

assert 2 * HEAD_DIM == LANES and WINDOW == BLOCK == LANES


def _rms(x, g):
    return x * lax.rsqrt(jnp.mean(x * x, axis=-1, keepdims=True) + RMS_EPS) * g


def _params(n_grid_dims):
    return pltpu.CompilerParams(dimension_semantics=("arbitrary",) * n_grid_dims,
                                vmem_limit_bytes=VMEM_LIMIT_BYTES)


def _resident(arr, *lead):
    rest = arr.shape[len(lead):]
    index = tuple(lead) + (0,) * len(rest)
    return pl.BlockSpec((None,) * len(lead) + rest, lambda *_: index, pipeline_mode=pl.Buffered(1))


def _mod_spec(mod_all, layer):
    return pl.BlockSpec((None, None) + mod_all.shape[2:], lambda b, t: (layer, b, 0, 0))


def _mod_body(c_ref, w_ref, b_ref, o_ref):
    c = c_ref[...]
    a = (c * jax.nn.sigmoid(c)).astype(BF16)
    o_ref[...] = jnp.dot(a, w_ref[...].astype(BF16), preferred_element_type=F32) + b_ref[...]


def _modulation(cs, w_mod, b_mod):
    depth, d, n = w_mod.shape
    tn = n // 4
    return pl.pallas_call(
        _mod_body,
        grid=(depth, n // tn),
        in_specs=[pl.BlockSpec((MOD_PAD_ROWS, d), lambda l, j: (0, 0)),
                  pl.BlockSpec((None, d, tn), lambda l, j: (l, 0, j)),
                  pl.BlockSpec((None, 1, tn), lambda l, j: (l, 0, j))],
        out_specs=pl.BlockSpec((None, MOD_PAD_ROWS, tn), lambda l, j: (l, 0, j)),
        out_shape=jax.ShapeDtypeStruct((depth, MOD_PAD_ROWS, n), F32),
        compiler_params=_params(2),
        name="modulation",
    )(cs, w_mod, b_mod.reshape(depth, 1, n))


def _slot_stream(lat, ctx, tm, n_latent):
    width = lat.shape[-1]
    if ctx is None:
        return [pl.BlockSpec((None, tm, width), lambda b, t: (b, t, 0))], [lat]
    return ([pl.BlockSpec((None, tm, width), lambda b, t: (jnp.minimum(b, n_latent - 1), t, 0)),
             pl.BlockSpec((None, tm, width), lambda b, t: (0, jnp.where(b == n_latent, t, 0), 0),
                          pipeline_mode=pl.Buffered(1))], [lat, ctx])


def _ffn_body(*refs, k, gi, groups, n_latent, n_x, n_a, n_casts):
    refs = list(refs)
    x_refs, refs = refs[:n_x], refs[n_x:]
    a_refs, refs = refs[:n_a], refs[n_a:]
    wo_ref = refs.pop(0) if n_a else None
    mod_ref, g_ref, wgu_ref, wd_ref = refs[:4]
    cast_src, refs = refs[4:4 + n_casts], refs[4 + n_casts:]
    o_ref = refs.pop(0)
    cast_dst, (h_ref, act_ref) = refs[:n_casts], refs[n_casts:]
    for src, dst in zip(cast_src, cast_dst):
        dst[...] = src[...].astype(BF16)
    in_ctx_slot = pl.program_id(0) == n_latent

    def pick(stream, rs):
        lat = stream[0][rs, :]
        return lat if len(stream) == 1 else jnp.where(in_ctx_slot, stream[1][rs, :], lat)

    shift, scale, gate = mod_ref[k:k + 1, :], mod_ref[k + 1:k + 2, :], mod_ref[k + 2:k + 3, :]
    rows = o_ref.shape[0] // groups
    for r in range(groups):
        rs = slice(r * rows, (r + 1) * rows)
        x = pick(x_refs, rs)
        if n_a:
            mixed = jnp.dot(pick(a_refs, rs), wo_ref[...], preferred_element_type=F32)
            x = x + mod_ref[5:6, :] * _rms(mixed, g_ref[3:4, :])
        h_ref[rs, :] = (_rms(x, g_ref[gi:gi + 1, :]) * (1.0 + scale) + shift).astype(BF16)
        for c in range(D_FF // FF_CHUNK):
            cs = slice(c * FF_CHUNK, (c + 1) * FF_CHUNK)
            us = slice(D_FF + c * FF_CHUNK, D_FF + (c + 1) * FF_CHUNK)
            gt = jnp.dot(h_ref[rs, :], wgu_ref[:, cs], preferred_element_type=F32)
            up = jnp.dot(h_ref[rs, :], wgu_ref[:, us], preferred_element_type=F32)
            act_ref[rs, cs] = (gt * jax.nn.sigmoid(gt) * up).astype(BF16)
        y = jnp.dot(act_ref[rs, :], wd_ref[...], preferred_element_type=F32)
        o_ref[rs, :] = x + HALF_STEP * gate * _rms(y, g_ref[gi + 1:gi + 2, :])


def _ffn(x, mod_all, norm_g, wgu, wd, casts, *, layer, half, n_slots, n_latent, tm, groups, attn=None):
    _, s, d = x[0].shape
    steps_per_slot = s // tm
    assert n_slots * steps_per_slot >= CAST_STEPS
    x_specs, x_ops = _slot_stream(*x, tm, n_latent)
    a_specs, a_ops = [], []
    if attn is not None:
        a_specs, a_ops = _slot_stream(*attn[0], tm, n_latent)
        a_specs, a_ops = a_specs + [_resident(attn[1])], a_ops + [attn[1]]
    body = functools.partial(_ffn_body, k=6 * half, gi=4 * half, groups=groups, n_latent=n_latent,
                             n_x=len(x_ops), n_a=max(len(a_ops) - 1, 0), n_casts=len(casts))

    def cast_block(b, t):
        return jnp.minimum(b * steps_per_slot + t, CAST_STEPS - 1)

    cast_in, cast_out, cast_shapes = [], [], []
    for src, lead in casts:
        r, c = src.shape[len(lead):]
        rows = r // CAST_STEPS
        assert r % CAST_STEPS == 0 and rows % BF16_SUBLANES == 0
        cast_in.append(pl.BlockSpec((None,) * len(lead) + (rows, c),
                                    lambda b, t, lead=tuple(lead): lead + (cast_block(b, t), 0)))
        cast_out.append(pl.BlockSpec((rows, c), lambda b, t: (cast_block(b, t), 0)))
        cast_shapes.append(jax.ShapeDtypeStruct((r, c), BF16))
    outs = pl.pallas_call(
        body,
        grid=(n_slots, steps_per_slot),
        in_specs=x_specs + a_specs + [_mod_spec(mod_all, layer), _resident(norm_g, layer), _resident(wgu),
                                      _resident(wd)] + cast_in,
        out_specs=[pl.BlockSpec((None, tm, d), lambda b, t: (b, t, 0))] + cast_out,
        out_shape=[jax.ShapeDtypeStruct((n_slots, s, d), F32)] + cast_shapes,
        scratch_shapes=[pltpu.VMEM((tm, d), BF16), pltpu.VMEM((tm, D_FF), BF16)],
        compiler_params=_params(2),
        name="ffn",
    )(*x_ops, *a_ops, mod_all, norm_g, wgu, wd, *[src for src, _ in casts])
    return outs[0], outs[1:]


def _conv_body(x_ref, xp_ref, xn_ref, mod_ref, g_ref, win_ref, cw_ref, wout_ref, o_ref,
               h_ref, gy_ref, *, tm, n_latent, seq, ctx_len):
    halo = BF16_SUBLANES
    rows = tm + 2 * halo
    slot, t = pl.program_id(0), pl.program_id(1)
    shift, scale, gate = mod_ref[3:4, :], mod_ref[4:5, :], mod_ref[5:6, :]
    g_in = g_ref[2:3, :]

    def modulated(v):
        return (_rms(v, g_in) * (1.0 + scale) + shift).astype(BF16)

    x = x_ref[...]
    h_ref[0:tm, :] = modulated(x)
    h_ref[tm:tm + halo, :] = modulated(xn_ref[...])
    h_ref[tm + halo:rows, :] = modulated(xp_ref[...])

    period = jnp.where(slot == n_latent, ctx_len, seq)
    pos = (t * tm + lax.broadcasted_iota(jnp.int32, (tm, 1), 0)) & (period - 1)
    has_prev = pos != 0
    has_next = pos != period - 1

    for c in range(D_MODEL // CONV_CHUNK):
        cs = slice(c * CONV_CHUNK, (c + 1) * CONV_CHUNK)
        cc = slice(D_MODEL + c * CONV_CHUNK, D_MODEL + (c + 1) * CONV_CHUNK)
        uc = slice(2 * D_MODEL + c * CONV_CHUNK, 2 * D_MODEL + (c + 1) * CONV_CHUNK)
        bg = jnp.dot(h_ref[0:tm, :], win_ref[:, cs], preferred_element_type=F32)
        cu = (jnp.dot(h_ref[...], win_ref[:, cc], preferred_element_type=F32)
              * jnp.dot(h_ref[...], win_ref[:, uc], preferred_element_type=F32))
        prev = jnp.where(has_prev, pltpu.roll(cu, 1, 0)[0:tm, :], 0.0)
        nxt = jnp.where(has_next, pltpu.roll(cu, rows - 1, 0)[0:tm, :], 0.0)
        y = cw_ref[0:1, cs] * prev + cw_ref[1:2, cs] * cu[0:tm, :] + cw_ref[2:3, cs] * nxt
        gy_ref[:, cs] = (bg * y).astype(BF16)
    y = jnp.dot(gy_ref[...], wout_ref[...], preferred_element_type=F32)
    o_ref[...] = x + gate * _rms(y, g_ref[3:4, :])


def _conv(xz, mod_all, norm_g, win, cw, wout, *, layer, j, n_slots, n_latent, ctx_len, tm):
    _, s, d = xz.shape
    halo = BF16_SUBLANES
    hb = tm // halo
    last = s // halo - 1
    body = functools.partial(_conv_body, tm=tm, n_latent=n_latent, seq=s, ctx_len=ctx_len)
    return pl.pallas_call(
        body,
        grid=(n_slots, s // tm),
        in_specs=[pl.BlockSpec((None, tm, d), lambda b, t: (b, t, 0)),
                  pl.BlockSpec((None, halo, d), lambda b, t: (b, jnp.maximum(t * hb - 1, 0), 0)),
                  pl.BlockSpec((None, halo, d), lambda b, t: (b, jnp.minimum((t + 1) * hb, last), 0)),
                  _mod_spec(mod_all, layer), _resident(norm_g, layer), _resident(win),
                  _resident(cw, j), _resident(wout)],
        out_specs=pl.BlockSpec((None, tm, d), lambda b, t: (b, t, 0)),
        out_shape=jax.ShapeDtypeStruct((n_slots, s, d), F32),
        scratch_shapes=[pltpu.VMEM((tm + 2 * halo, d), BF16), pltpu.VMEM((tm, d), BF16)],
        compiler_params=_params(2),
        name="conv_mixer",
    )(xz, xz, xz, mod_all, norm_g, win, cw, wout)


def _qkv_body(x_ref, mod_ref, g_ref, w_ref, cos_ref, sin_ref, q_ref, kt_ref, v_ref, h_ref, *, q_scale):
    shift, scale = mod_ref[3:4, :], mod_ref[4:5, :]
    h_ref[...] = (_rms(x_ref[...], g_ref[2:3, :]) * (1.0 + scale) + shift).astype(BF16)
    cos, sin = cos_ref[...], sin_ref[...]
    first_half = (lax.broadcasted_iota(jnp.int32, (1, LANES), 1) & (2 * ROPE_QUARTER - 1)) < ROPE_QUARTER

    def rope(t):
        partner = jnp.where(first_half, pltpu.roll(t, LANES - ROPE_QUARTER, 1), pltpu.roll(t, ROPE_QUARTER, 1))
        return t * cos + partner * sin

    qk = jnp.dot(h_ref[...], w_ref[:, :Q_DIM + KV_DIM], preferred_element_type=F32)
    for j in range(Q_DIM // LANES):
        q_ref[:, j * LANES:(j + 1) * LANES] = (
            rope(qk[:, j * LANES:(j + 1) * LANES]) * q_scale).astype(BF16)
    for j in range(KV_DIM // LANES):
        kt_ref[j * LANES:(j + 1) * LANES, :] = rope(
            qk[:, Q_DIM + j * LANES:Q_DIM + (j + 1) * LANES]).T.astype(BF16)
    v = jnp.dot(h_ref[...], w_ref[:, Q_DIM + KV_DIM:], preferred_element_type=F32)
    heads = [v[:, h * HEAD_DIM:(h + 1) * HEAD_DIM] for h in range(N_KV_HEADS)]
    v_ref[...] = jnp.concatenate([vh for vh in heads for _ in range(2)], axis=1).astype(BF16)


def _qkv(xz, mod_all, norm_g, w, cos, sin, *, layer, n_latent, tm, q_scale):
    n_slots, s, d = xz.shape
    tok = lambda b, t: (b, t, 0)
    tab = lambda b, t: (jnp.where(b == n_latent, 1, 0), t, 0)
    return pl.pallas_call(
        functools.partial(_qkv_body, q_scale=q_scale),
        grid=(n_slots, s // tm),
        in_specs=[pl.BlockSpec((None, tm, d), tok),
                  _mod_spec(mod_all, layer), _resident(norm_g, layer), _resident(w),
                  pl.BlockSpec((None, tm, LANES), tab), pl.BlockSpec((None, tm, LANES), tab)],
        out_specs=[pl.BlockSpec((None, tm, Q_DIM), tok),
                   pl.BlockSpec((None, KV_DIM, tm), lambda b, t: (b, 0, t)),
                   pl.BlockSpec((None, tm, 2 * KV_DIM), tok)],
        out_shape=[jax.ShapeDtypeStruct((n_slots, s, Q_DIM), BF16),
                   jax.ShapeDtypeStruct((n_slots, KV_DIM, s), BF16),
                   jax.ShapeDtypeStruct((n_slots, s, 2 * KV_DIM), BF16)],
        scratch_shapes=[pltpu.VMEM((tm, d), BF16)],
        compiler_params=_params(2),
        name="qkv_rope",
    )(xz, mod_all, norm_g, w, cos, sin)


def _attn_body(sink_ref, q_ref, *refs, n_qb, window, seq, sink_row):
    if window:
        ktp_ref, ktc_ref, ktn_ref, ktz_ref, vp_ref, vc_ref, vn_ref, vz_ref, o_ref = refs
        kt_all = jnp.concatenate([ktp_ref[...], ktc_ref[...], ktn_ref[...]], axis=1)
        v_all = jnp.concatenate([vp_ref[...], vc_ref[...], vn_ref[...]], axis=0)
    else:
        ktz_ref, vz_ref, o_ref = refs
    t = pl.program_id(1)
    row = lax.broadcasted_iota(jnp.int32, (BLOCK, BLOCK), 0)
    col = lax.broadcasted_iota(jnp.int32, (BLOCK, BLOCK), 1)
    low_lanes = lax.broadcasted_iota(jnp.int32, (1, LANES), 1) < HEAD_DIM
    for i in range(n_qb):
        rows = slice(i * BLOCK, (i + 1) * BLOCK)
        if window:
            n = t * n_qb + i
            keep_prev = (col >= row) & (n > 0)
            keep_next = (col <= row) & (n < seq // BLOCK - 1)
        for h in range(N_KV_HEADS):
            hr = slice(h * HEAD_DIM, (h + 1) * HEAD_DIM)
            hl = slice(h * LANES, (h + 1) * LANES)
            if window:
                keys = jnp.concatenate([kt_all[hr, i * BLOCK:(i + 3) * BLOCK], ktz_ref[hr, :]], axis=1)
                vals = jnp.concatenate([v_all[i * BLOCK:(i + 3) * BLOCK, hl], vz_ref[:, hl]], axis=0)
            else:
                keys, vals = ktz_ref[hr, :], vz_ref[:, hl]
            nk = keys.shape[1]
            zk = jnp.zeros_like(keys)
            rhs = jnp.concatenate([jnp.concatenate([keys, zk], axis=1),
                                   jnp.concatenate([zk, keys], axis=1)], axis=0)
            low_v = lax.broadcasted_iota(jnp.int32, vals.shape, 1) < HEAD_DIM
            zv = jnp.zeros_like(vals)
            ones_lo = jnp.where(low_v, 1.0, 0.0).astype(BF16)
            ones_hi = jnp.where(low_v, 0.0, 1.0).astype(BF16)
            vv = jnp.concatenate([jnp.concatenate([jnp.where(low_v, vals, zv), ones_lo], axis=1),
                                  jnp.concatenate([jnp.where(low_v, zv, vals), ones_hi], axis=1)], axis=0)
            for jj in range(GROUP // 2):
                j = (GROUP // 2) * h + jj
                s = jnp.dot(q_ref[rows, j * LANES:(j + 1) * LANES], rhs, preferred_element_type=F32)
                ps, sink_terms = [], []
                for e in range(2):
                    sink = sink_ref[sink_row, 2 * j + e] * LOG2_E
                    se = s[:, e * nk:(e + 1) * nk]
                    if window:
                        se = jnp.concatenate(
                            [jnp.where(keep_prev, se[:, :BLOCK], NEG_INF), se[:, BLOCK:2 * BLOCK],
                             jnp.where(keep_next, se[:, 2 * BLOCK:3 * BLOCK], NEG_INF), se[:, 3 * BLOCK:]],
                            axis=1)
                    m = jnp.maximum(jnp.max(se, axis=-1, keepdims=True), sink)
                    ps.append(jnp.exp2(se - m).astype(BF16))
                    sink_terms.append(jnp.exp2(sink - m))
                ov = jnp.dot(jnp.concatenate(ps, axis=1), vv, preferred_element_type=F32)
                denom = ov[:, LANES:] + jnp.where(low_lanes, sink_terms[0], sink_terms[1])
                o_ref[rows, j * LANES:(j + 1) * LANES] = (ov[:, :LANES] / denom).astype(BF16)


def _attention_latent(q, kt, v2, sink, *, j, n_latent, ctx_len, tq):
    _, s, _ = q.shape
    assert tq % BLOCK == 0 and s % tq == 0
    qpb = tq // BLOCK
    nb = s // BLOCK
    body = functools.partial(_attn_body, n_qb=qpb, window=True, seq=s, sink_row=j)
    kt_spec = lambda w, f: pl.BlockSpec((None, KV_DIM, w), f)
    v_spec = lambda w, f: pl.BlockSpec((None, w, 2 * KV_DIM), f)
    return pl.pallas_call(
        body,
        grid=(n_latent, s // tq),
        in_specs=[pl.BlockSpec(memory_space=pltpu.SMEM),
                  pl.BlockSpec((None, tq, Q_DIM), lambda b, t: (b, t, 0)),
                  kt_spec(BLOCK, lambda b, t: (b, 0, jnp.maximum(t * qpb - 1, 0))),
                  kt_spec(tq, lambda b, t: (b, 0, t)),
                  kt_spec(BLOCK, lambda b, t: (b, 0, jnp.minimum((t + 1) * qpb, nb - 1))),
                  kt_spec(ctx_len, lambda b, t: (n_latent, 0, b)),
                  v_spec(BLOCK, lambda b, t: (b, jnp.maximum(t * qpb - 1, 0), 0)),
                  v_spec(tq, lambda b, t: (b, t, 0)),
                  v_spec(BLOCK, lambda b, t: (b, jnp.minimum((t + 1) * qpb, nb - 1), 0)),
                  v_spec(ctx_len, lambda b, t: (n_latent, b, 0))],
        out_specs=pl.BlockSpec((None, tq, Q_DIM), lambda b, t: (b, t, 0)),
        out_shape=jax.ShapeDtypeStruct((n_latent, s, Q_DIM), BF16),
        compiler_params=_params(2),
        name="window_attention",
    )(sink, q, kt, kt, kt, kt, v2, v2, v2, v2)


def _attention_context(q, kt, v2, sink, *, j, n_latent, ctx_len):
    _, s, _ = q.shape
    assert ctx_len % BLOCK == 0
    body = functools.partial(_attn_body, n_qb=ctx_len // BLOCK, window=False, seq=s, sink_row=j)
    return pl.pallas_call(
        body,
        grid=(1, n_latent),
        in_specs=[pl.BlockSpec(memory_space=pltpu.SMEM),
                  pl.BlockSpec((None, ctx_len, Q_DIM), lambda _, b: (n_latent, b, 0)),
                  pl.BlockSpec((None, KV_DIM, ctx_len), lambda _, b: (n_latent, 0, b)),
                  pl.BlockSpec((None, ctx_len, 2 * KV_DIM), lambda _, b: (n_latent, b, 0))],
        out_specs=pl.BlockSpec((None, ctx_len, Q_DIM), lambda _, b: (0, b, 0)),
        out_shape=jax.ShapeDtypeStruct((1, s, Q_DIM), BF16),
        compiler_params=_params(2),
        name="context_attention",
    )(sink, q, kt, v2)


def _rope_tables(seq):
    rows = seq // GRID_W
    row = jnp.broadcast_to(jnp.arange(rows)[:, None], (rows, GRID_W)).reshape(-1).astype(F32)
    col = jnp.broadcast_to(jnp.arange(GRID_W)[None, :], (rows, GRID_W)).reshape(-1).astype(F32)
    inv_freq = ROPE_THETA ** (-jnp.arange(ROPE_QUARTER, dtype=F32) / ROPE_QUARTER)
    ang_r, ang_c = row[:, None] * inv_freq, col[:, None] * inv_freq
    cos = jnp.concatenate([jnp.cos(ang_r)] * 2 + [jnp.cos(ang_c)] * 2, axis=1)
    sin = jnp.concatenate([-jnp.sin(ang_r), jnp.sin(ang_r), -jnp.sin(ang_c), jnp.sin(ang_c)], axis=1)
    reps = LANES // HEAD_DIM
    cos, sin = jnp.tile(cos, (1, reps)), jnp.tile(sin, (1, reps))
    return jnp.stack([cos, jnp.ones_like(cos)]), jnp.stack([sin, jnp.zeros_like(sin)])


def kernel(x, c, ctx, c_ctx, w_mod, b_mod, norm_g, ffn_w_gu, ffn_w_down, conv_w_in, conv_w, conv_w_out,
           attn_w_qkv, attn_w_o, attn_sink):
    n_latent, seq, d = x.shape
    ctx_len = ctx.shape[1]
    assert d == D_MODEL and n_latent * ctx_len == seq and seq % GRID_W == 0
    assert ctx_len & (ctx_len - 1) == 0 and seq & (seq - 1) == 0
    n_slots = n_latent + 1

    cs = jnp.concatenate([c, c_ctx[None, :], jnp.zeros((MOD_PAD_ROWS - n_slots, d), F32)], axis=0)
    mod_all = _modulation(cs, w_mod, b_mod)[:, :n_slots].reshape(DEPTH, n_slots, N_MOD, d)
    cos, sin = _rope_tables(seq)

    wgu, wd = ffn_w_gu[0, 0].astype(BF16), ffn_w_down[0, 0].astype(BF16)
    xz, ctx_slot = x, ctx.reshape(1, seq, d)
    for i in range(DEPTH):
        last = i == DEPTH - 1
        use_attn = (i % N_MIXERS) == 1
        j = i // N_MIXERS
        live = n_latent if last else n_slots

        mixer_w = ([(attn_w_qkv, (j,)), (attn_w_o, (j,))] if use_attn
                   else [(conv_w_in, (j,)), (conv_w_out, (j,))])
        if i == 0:
            mix_in, mix_out = (w[lead].astype(BF16) for w, lead in mixer_w)
            mixer_w = []
        xz, cast_w = _ffn(
            (xz, ctx_slot), mod_all, norm_g, wgu, wd, [(ffn_w_gu, (i, 1)), (ffn_w_down, (i, 1))] + mixer_w,
            layer=i, half=0, n_slots=n_slots if (use_attn or not last) else n_latent, n_latent=n_latent,
            tm=FFN_ROWS, groups=FFN_ROW_GROUPS)
        wgu, wd = cast_w[:2]
        if mixer_w:
            mix_in, mix_out = cast_w[2:]
        ctx_slot = None
        attn = None
        if use_attn:
            q, kt, v2 = _qkv(xz, mod_all, norm_g, mix_in, cos, sin, layer=i, n_latent=n_latent, tm=MIX_ROWS,
                             q_scale=HEAD_DIM ** -0.5 * LOG2_E)
            o = _attention_latent(q, kt, v2, attn_sink, j=j, n_latent=n_latent, ctx_len=ctx_len, tq=ATTN_ROWS)
            oz = None if last else _attention_context(q, kt, v2, attn_sink, j=j, n_latent=n_latent,
                                                      ctx_len=ctx_len)
            attn = ((o, oz), mix_out)
        else:
            xz = _conv(xz, mod_all, norm_g, mix_in, conv_w, mix_out, layer=i, j=j, n_slots=live,
                       n_latent=n_latent, ctx_len=ctx_len, tm=MIX_ROWS)
        next_ffn = [] if last else [(ffn_w_gu, (i + 1, 0)), (ffn_w_down, (i + 1, 0))]
        xz, next_w = _ffn((xz, None), mod_all, norm_g, wgu, wd, next_ffn, layer=i, half=1, n_slots=live,
                          n_latent=n_latent, tm=FFN_ROWS, groups=FFN_ROW_GROUPS, attn=attn)
        if not last:
            wgu, wd = next_w
    return xz
```

```python
import functools

import jax
import jax.numpy as jnp
from jax import lax
from jax.experimental import pallas as pl
from jax.experimental.pallas import tpu as pltpu

D_MODEL = 1024
DEPTH = 4
GRID_W = 64
N_MIXERS = 2
N_HEADS = 16
N_KV_HEADS = 4
HEAD_DIM = D_MODEL // N_HEADS
GROUP = N_HEADS // N_KV_HEADS
Q_DIM = N_HEADS * HEAD_DIM
KV_DIM = N_KV_HEADS * HEAD_DIM
WINDOW = 128
BLOCK = 128
ROPE_THETA = 10000.0
ROPE_QUARTER = HEAD_DIM // 4
CONV_WIDTH = 3
D_FF = 2816
N_MOD = 9
RMS_EPS = 1e-6
NEG_INF = -1e30
HALF_STEP = 0.5
LOG2_E = 1.4426950408889634

LANES = 128
BF16_SUBLANES = 16
VMEM_LIMIT_BYTES = 56 * 1024 * 1024

FF_CHUNK = 256
CONV_CHUNK = 256
MOD_PAD_ROWS = 16
FFN_ROWS = 1024
FFN_ROW_GROUPS = 4
FFN_FINISH_AT = 2
FFN_START_NEXT_AT = 5
MIX_ROWS = 1024
MIX_ROW_GROUPS = 4
ATTN_ROWS = 4 * BLOCK
ATTN_LOOKAHEAD = 1
CAST_STEPS = 16

F32 = jnp.float32
BF16 = jnp.bfloat16

assert 2 * HEAD_DIM == LANES and WINDOW == BLOCK == LANES


def _rms(x, g):
    return x * lax.rsqrt(jnp.mean(x * x, axis=-1, keepdims=True) + RMS_EPS) * g


def _params(n_grid_dims):
    return pltpu.CompilerParams(dimension_semantics=("arbitrary",) * n_grid_dims,
                                vmem_limit_bytes=VMEM_LIMIT_BYTES)


def _resident(arr, *lead):
    rest = arr.shape[len(lead):]
    index = tuple(lead) + (0,) * len(rest)
    return pl.BlockSpec((None,) * len(lead) + rest, lambda *_: index, pipeline_mode=pl.Buffered(1))


def _mod_spec(mod_all, layer):
    return pl.BlockSpec((None, None) + mod_all.shape[2:], lambda b, t: (layer, b, 0, 0))


def _mod_body(c_ref, w_ref, b_ref, o_ref):
    c = c_ref[...]
    a = (c * jax.nn.sigmoid(c)).astype(BF16)
    o_ref[...] = jnp.dot(a, w_ref[...].astype(BF16), preferred_element_type=F32) + b_ref[...]


def _modulation(cs, w_mod, b_mod):
    depth, d, n = w_mod.shape
    tn = n // 4
    return pl.pallas_call(
        _mod_body,
        grid=(depth, n // tn),
        in_specs=[pl.BlockSpec((MOD_PAD_ROWS, d), lambda l, j: (0, 0)),
                  pl.BlockSpec((None, d, tn), lambda l, j: (l, 0, j)),
                  pl.BlockSpec((None, 1, tn), lambda l, j: (l, 0, j))],
        out_specs=pl.BlockSpec((None, MOD_PAD_ROWS, tn), lambda l, j: (l, 0, j)),
        out_shape=jax.ShapeDtypeStruct((depth, MOD_PAD_ROWS, n), F32),
        compiler_params=_params(2),
        name="modulation",
    )(cs, w_mod, b_mod.reshape(depth, 1, n))


def _slot_stream(lat, ctx, tm, n_latent):
    width = lat.shape[-1]
    if ctx is None:
        return [pl.BlockSpec((None, tm, width), lambda b, t: (b, t, 0))], [lat]
    return ([pl.BlockSpec((None, tm, width), lambda b, t: (jnp.minimum(b, n_latent - 1), t, 0)),
             pl.BlockSpec((None, tm, width), lambda b, t: (0, jnp.where(b == n_latent, t, 0), 0),
                          pipeline_mode=pl.Buffered(1))], [lat, ctx])


def _ffn_body(*refs, k, gi, groups, n_latent, n_x, n_a, n_casts):
    refs = list(refs)
    x_refs, refs = refs[:n_x], refs[n_x:]
    a_refs, refs = refs[:n_a], refs[n_a:]
    wo_ref = refs.pop(0) if n_a else None
    mod_ref, g_ref, wgu_ref, wd_ref = refs[:4]
    cast_src, refs = refs[4:4 + n_casts], refs[4 + n_casts:]
    o_ref = refs.pop(0)
    cast_dst, (h_ref, act_ref) = refs[:n_casts], refs[n_casts:]
    for src, dst in zip(cast_src, cast_dst):
        dst[...] = src[...].astype(BF16)
    in_ctx_slot = pl.program_id(0) == n_latent

    def pick(stream, rs):
        lat = stream[0][rs, :]
        return lat if len(stream) == 1 else jnp.where(in_ctx_slot, stream[1][rs, :], lat)

    shift, scale, gate = mod_ref[k:k + 1, :], mod_ref[k + 1:k + 2, :], mod_ref[k + 2:k + 3, :]
    rows = o_ref.shape[0] // groups

    def finish(rs, x):
        y = jnp.dot(act_ref[rs, :], wd_ref[...], preferred_element_type=F32)
        o_ref[rs, :] = x + HALF_STEP * gate * _rms(y, g_ref[gi + 1:gi + 2, :])

    def start(r):
        rs = slice(r * rows, (r + 1) * rows)
        x = pick(x_refs, rs)
        if n_a:
            mixed = jnp.dot(pick(a_refs, rs), wo_ref[...], preferred_element_type=F32)
            x = x + mod_ref[5:6, :] * _rms(mixed, g_ref[3:4, :])
        h_ref[rs, :] = (_rms(x, g_ref[gi:gi + 1, :]) * (1.0 + scale) + shift).astype(BF16)
        return rs, x

    skew_a, skew_b = FFN_FINISH_AT, FFN_START_NEXT_AT
    pending = None
    cur = start(0)
    for r in range(groups):
        rs, x = cur
        nxt = None
        for c in range(D_FF // FF_CHUNK):
            cs = slice(c * FF_CHUNK, (c + 1) * FF_CHUNK)
            us = slice(D_FF + c * FF_CHUNK, D_FF + (c + 1) * FF_CHUNK)
            gt = jnp.dot(h_ref[rs, :], wgu_ref[:, cs], preferred_element_type=F32)
            up = jnp.dot(h_ref[rs, :], wgu_ref[:, us], preferred_element_type=F32)
            act_ref[rs, cs] = (gt * jax.nn.sigmoid(gt) * up).astype(BF16)
            if c == skew_a and pending is not None:
                finish(*pending)
            if c == skew_b and r + 1 < groups:
                nxt = start(r + 1)
        pending = cur
        if r + 1 < groups:
            cur = nxt if nxt is not None else start(r + 1)
    finish(*pending)


def _ffn(x, mod_all, norm_g, wgu, wd, casts, *, layer, half, n_slots, n_latent, tm, groups, attn=None):
    _, s, d = x[0].shape
    steps_per_slot = s // tm
    assert n_slots * steps_per_slot >= CAST_STEPS
    x_specs, x_ops = _slot_stream(*x, tm, n_latent)
    a_specs, a_ops = [], []
    if attn is not None:
        a_specs, a_ops = _slot_stream(*attn[0], tm, n_latent)
        a_specs, a_ops = a_specs + [_resident(attn[1])], a_ops + [attn[1]]
    body = functools.partial(_ffn_body, k=6 * half, gi=4 * half, groups=groups, n_latent=n_latent,
                             n_x=len(x_ops), n_a=max(len(a_ops) - 1, 0), n_casts=len(casts))

    def cast_block(b, t):
        return jnp.minimum(b * steps_per_slot + t, CAST_STEPS - 1)

    cast_in, cast_out, cast_shapes = [], [], []
    for src, lead in casts:
        r, c = src.shape[len(lead):]
        rows = r // CAST_STEPS
        assert r % CAST_STEPS == 0 and rows % BF16_SUBLANES == 0
        cast_in.append(pl.BlockSpec((None,) * len(lead) + (rows, c),
                                    lambda b, t, lead=tuple(lead): lead + (cast_block(b, t), 0)))
        cast_out.append(pl.BlockSpec((rows, c), lambda b, t: (cast_block(b, t), 0)))
        cast_shapes.append(jax.ShapeDtypeStruct((r, c), BF16))
    outs = pl.pallas_call(
        body,
        grid=(n_slots, steps_per_slot),
        in_specs=x_specs + a_specs + [_mod_spec(mod_all, layer), _resident(norm_g, layer), _resident(wgu),
                                      _resident(wd)] + cast_in,
        out_specs=[pl.BlockSpec((None, tm, d), lambda b, t: (b, t, 0))] + cast_out,
        out_shape=[jax.ShapeDtypeStruct((n_slots, s, d), F32)] + cast_shapes,
        scratch_shapes=[pltpu.VMEM((tm, d), BF16), pltpu.VMEM((tm, D_FF), BF16)],
        compiler_params=_params(2),
        name="ffn",
    )(*x_ops, *a_ops, mod_all, norm_g, wgu, wd, *[src for src, _ in casts])
    return outs[0], outs[1:]


def _conv_body(x_ref, xp_ref, xn_ref, mod_ref, g_ref, win_ref, cw_ref, wout_ref, o_ref,
               h_ref, gy_ref, *, tm, n_latent, seq, ctx_len):
    halo = BF16_SUBLANES
    rows = tm + 2 * halo
    slot, t = pl.program_id(0), pl.program_id(1)
    shift, scale, gate = mod_ref[3:4, :], mod_ref[4:5, :], mod_ref[5:6, :]
    g_in = g_ref[2:3, :]

    def modulated(v):
        return (_rms(v, g_in) * (1.0 + scale) + shift).astype(BF16)

    grp = tm // MIX_ROW_GROUPS
    groups = [slice(r * grp, (r + 1) * grp) for r in range(MIX_ROW_GROUPS)]
    h_ref[0:tm, :] = modulated(x_ref[...])
    h_ref[tm:tm + halo, :] = modulated(xn_ref[...])
    h_ref[tm + halo:rows, :] = modulated(xp_ref[...])

    period = jnp.where(slot == n_latent, ctx_len, seq)
    pos = (t * tm + lax.broadcasted_iota(jnp.int32, (tm, 1), 0)) & (period - 1)
    has_prev = pos != 0
    has_next = pos != period - 1

    for c in range(D_MODEL // CONV_CHUNK):
        cs = slice(c * CONV_CHUNK, (c + 1) * CONV_CHUNK)
        cc = slice(D_MODEL + c * CONV_CHUNK, D_MODEL + (c + 1) * CONV_CHUNK)
        uc = slice(2 * D_MODEL + c * CONV_CHUNK, 2 * D_MODEL + (c + 1) * CONV_CHUNK)
        bg = jnp.dot(h_ref[0:tm, :], win_ref[:, cs], preferred_element_type=F32)
        cu = (jnp.dot(h_ref[...], win_ref[:, cc], preferred_element_type=F32)
              * jnp.dot(h_ref[...], win_ref[:, uc], preferred_element_type=F32))
        prev = jnp.where(has_prev, pltpu.roll(cu, 1, 0)[0:tm, :], 0.0)
        nxt = jnp.where(has_next, pltpu.roll(cu, rows - 1, 0)[0:tm, :], 0.0)
        y = cw_ref[0:1, cs] * prev + cw_ref[1:2, cs] * cu[0:tm, :] + cw_ref[2:3, cs] * nxt
        gy_ref[:, cs] = (bg * y).astype(BF16)
    for rs in groups:
        y = jnp.dot(gy_ref[rs, :], wout_ref[...], preferred_element_type=F32)
        o_ref[rs, :] = x_ref[rs, :] + gate * _rms(y, g_ref[3:4, :])


def _conv(xz, mod_all, norm_g, win, cw, wout, *, layer, j, n_slots, n_latent, ctx_len, tm):
    _, s, d = xz.shape
    halo = BF16_SUBLANES
    hb = tm // halo
    last = s // halo - 1
    body = functools.partial(_conv_body, tm=tm, n_latent=n_latent, seq=s, ctx_len=ctx_len)
    return pl.pallas_call(
        body,
        grid=(n_slots, s // tm),
        in_specs=[pl.BlockSpec((None, tm, d), lambda b, t: (b, t, 0)),
                  pl.BlockSpec((None, halo, d), lambda b, t: (b, jnp.maximum(t * hb - 1, 0), 0)),
                  pl.BlockSpec((None, halo, d), lambda b, t: (b, jnp.minimum((t + 1) * hb, last), 0)),
                  _mod_spec(mod_all, layer), _resident(norm_g, layer), _resident(win),
                  _resident(cw, j), _resident(wout)],
        out_specs=pl.BlockSpec((None, tm, d), lambda b, t: (b, t, 0)),
        out_shape=jax.ShapeDtypeStruct((n_slots, s, d), F32),
        scratch_shapes=[pltpu.VMEM((tm + 2 * halo, d), BF16), pltpu.VMEM((tm, d), BF16)],
        compiler_params=_params(2),
        name="conv_mixer",
    )(xz, xz, xz, mod_all, norm_g, win, cw, wout)


def _qkv_body(x_ref, mod_ref, g_ref, w_ref, cos_ref, sin_ref, q_ref, kt_ref, v_ref, h_ref, *, q_scale):
    shift, scale = mod_ref[3:4, :], mod_ref[4:5, :]
    h_ref[...] = (_rms(x_ref[...], g_ref[2:3, :]) * (1.0 + scale) + shift).astype(BF16)
    cos, sin = cos_ref[...], sin_ref[...]
    first_half = (lax.broadcasted_iota(jnp.int32, (1, LANES), 1) & (2 * ROPE_QUARTER - 1)) < ROPE_QUARTER

    def rope(t):
        partner = jnp.where(first_half, pltpu.roll(t, LANES - ROPE_QUARTER, 1), pltpu.roll(t, ROPE_QUARTER, 1))
        return t * cos + partner * sin

    qk = jnp.dot(h_ref[...], w_ref[:, :Q_DIM + KV_DIM], preferred_element_type=F32)
    for j in range(Q_DIM // LANES):
        q_ref[:, j * LANES:(j + 1) * LANES] = (
            rope(qk[:, j * LANES:(j + 1) * LANES]) * q_scale).astype(BF16)
    for j in range(KV_DIM // LANES):
        kt_ref[j * LANES:(j + 1) * LANES, :] = rope(
            qk[:, Q_DIM + j * LANES:Q_DIM + (j + 1) * LANES]).T.astype(BF16)
    v = jnp.dot(h_ref[...], w_ref[:, Q_DIM + KV_DIM:], preferred_element_type=F32)
    heads = [v[:, h * HEAD_DIM:(h + 1) * HEAD_DIM] for h in range(N_KV_HEADS)]
    v_ref[...] = jnp.concatenate([vh for vh in heads for _ in range(2)], axis=1).astype(BF16)


def _qkv(xz, mod_all, norm_g, w, cos, sin, *, layer, n_latent, tm, q_scale):
    n_slots, s, d = xz.shape
    tok = lambda b, t: (b, t, 0)
    tab = lambda b, t: (jnp.where(b == n_latent, 1, 0), t, 0)
    return pl.pallas_call(
        functools.partial(_qkv_body, q_scale=q_scale),
        grid=(n_slots, s // tm),
        in_specs=[pl.BlockSpec((None, tm, d), tok),
                  _mod_spec(mod_all, layer), _resident(norm_g, layer), _resident(w),
                  pl.BlockSpec((None, tm, LANES), tab), pl.BlockSpec((None, tm, LANES), tab)],
        out_specs=[pl.BlockSpec((None, tm, Q_DIM), tok),
                   pl.BlockSpec((None, KV_DIM, tm), lambda b, t: (b, 0, t)),
                   pl.BlockSpec((None, tm, 2 * KV_DIM), tok)],
        out_shape=[jax.ShapeDtypeStruct((n_slots, s, Q_DIM), BF16),
                   jax.ShapeDtypeStruct((n_slots, KV_DIM, s), BF16),
                   jax.ShapeDtypeStruct((n_slots, s, 2 * KV_DIM), BF16)],
        scratch_shapes=[pltpu.VMEM((tm, d), BF16)],
        compiler_params=_params(2),
        name="qkv_rope",
    )(xz, mod_all, norm_g, w, cos, sin)


def _attn_body(sink_ref, q_ref, *refs, n_qb, window, seq, sink_row):
    if window:
        ktp_ref, ktc_ref, ktn_ref, ktz_ref, vp_ref, vc_ref, vn_ref, vz_ref, o_ref = refs
        kt_all = jnp.concatenate([ktp_ref[...], ktc_ref[...], ktn_ref[...]], axis=1)
        v_all = jnp.concatenate([vp_ref[...], vc_ref[...], vn_ref[...]], axis=0)
    else:
        ktz_ref, vz_ref, o_ref = refs
    t = pl.program_id(1)
    row = lax.broadcasted_iota(jnp.int32, (BLOCK, BLOCK), 0)
    col = lax.broadcasted_iota(jnp.int32, (BLOCK, BLOCK), 1)
    low_lanes = lax.broadcasted_iota(jnp.int32, (1, LANES), 1) < HEAD_DIM
    def operands(i, h):
        hr = slice(h * HEAD_DIM, (h + 1) * HEAD_DIM)
        hl = slice(h * LANES, (h + 1) * LANES)
        if window:
            keys = jnp.concatenate([kt_all[hr, i * BLOCK:(i + 3) * BLOCK], ktz_ref[hr, :]], axis=1)
            vals = jnp.concatenate([v_all[i * BLOCK:(i + 3) * BLOCK, hl], vz_ref[:, hl]], axis=0)
        else:
            keys, vals = ktz_ref[hr, :], vz_ref[:, hl]
        zk = jnp.zeros_like(keys)
        rhs = jnp.concatenate([jnp.concatenate([keys, zk], axis=1),
                               jnp.concatenate([zk, keys], axis=1)], axis=0)
        low_v = lax.broadcasted_iota(jnp.int32, vals.shape, 1) < HEAD_DIM
        zv = jnp.zeros_like(vals)
        ones_lo = jnp.where(low_v, 1.0, 0.0).astype(BF16)
        ones_hi = jnp.where(low_v, 0.0, 1.0).astype(BF16)
        vv = jnp.concatenate([jnp.concatenate([jnp.where(low_v, vals, zv), ones_lo], axis=1),
                              jnp.concatenate([jnp.where(low_v, zv, vals), ones_hi], axis=1)], axis=0)
        return rhs, vv

    def scores(i, j, rhs):
        return jnp.dot(q_ref[i * BLOCK:(i + 1) * BLOCK, j * LANES:(j + 1) * LANES], rhs,
                       preferred_element_type=F32)

    def finish(i, j, s, vv):
        nk = s.shape[1] // 2
        if window:
            n = t * n_qb + i
            keep_prev = (col >= row) & (n > 0)
            keep_next = (col <= row) & (n < seq // BLOCK - 1)
        ps, sink_terms = [], []
        for e in range(2):
            sink = sink_ref[sink_row, 2 * j + e] * LOG2_E
            se = s[:, e * nk:(e + 1) * nk]
            if window:
                se = jnp.concatenate(
                    [jnp.where(keep_prev, se[:, :BLOCK], NEG_INF), se[:, BLOCK:2 * BLOCK],
                     jnp.where(keep_next, se[:, 2 * BLOCK:3 * BLOCK], NEG_INF), se[:, 3 * BLOCK:]], axis=1)
            m = jnp.maximum(jnp.max(se, axis=-1, keepdims=True), sink)
            ps.append(jnp.exp2(se - m).astype(BF16))
            sink_terms.append(jnp.exp2(sink - m))
        ov = jnp.dot(jnp.concatenate(ps, axis=1), vv, preferred_element_type=F32)
        denom = ov[:, LANES:] + jnp.where(low_lanes, sink_terms[0], sink_terms[1])
        o_ref[i * BLOCK:(i + 1) * BLOCK, j * LANES:(j + 1) * LANES] = (ov[:, :LANES] / denom).astype(BF16)

    in_flight = []
    for i in range(n_qb):
        for h in range(N_KV_HEADS):
            rhs, vv = operands(i, h)
            for jj in range(GROUP // 2):
                j = (GROUP // 2) * h + jj
                in_flight.append((i, j, scores(i, j, rhs), vv))
                if len(in_flight) > ATTN_LOOKAHEAD:
                    finish(*in_flight.pop(0))
    for item in in_flight:
        finish(*item)


def _attention_latent(q, kt, v2, sink, *, j, n_latent, ctx_len, tq):
    _, s, _ = q.shape
    assert tq % BLOCK == 0 and s % tq == 0
    qpb = tq // BLOCK
    nb = s // BLOCK
    body = functools.partial(_attn_body, n_qb=qpb, window=True, seq=s, sink_row=j)
    kt_spec = lambda w, f: pl.BlockSpec((None, KV_DIM, w), f)
    v_spec = lambda w, f: pl.BlockSpec((None, w, 2 * KV_DIM), f)
    return pl.pallas_call(
        body,
        grid=(n_latent, s // tq),
        in_specs=[pl.BlockSpec(memory_space=pltpu.SMEM),
                  pl.BlockSpec((None, tq, Q_DIM), lambda b, t: (b, t, 0)),
                  kt_spec(BLOCK, lambda b, t: (b, 0, jnp.maximum(t * qpb - 1, 0))),
                  kt_spec(tq, lambda b, t: (b, 0, t)),
                  kt_spec(BLOCK, lambda b, t: (b, 0, jnp.minimum((t + 1) * qpb, nb - 1))),
                  kt_spec(ctx_len, lambda b, t: (n_latent, 0, b)),
                  v_spec(BLOCK, lambda b, t: (b, jnp.maximum(t * qpb - 1, 0), 0)),
                  v_spec(tq, lambda b, t: (b, t, 0)),
                  v_spec(BLOCK, lambda b, t: (b, jnp.minimum((t + 1) * qpb, nb - 1), 0)),
                  v_spec(ctx_len, lambda b, t: (n_latent, b, 0))],
        out_specs=pl.BlockSpec((None, tq, Q_DIM), lambda b, t: (b, t, 0)),
        out_shape=jax.ShapeDtypeStruct((n_latent, s, Q_DIM), BF16),
        compiler_params=_params(2),
        name="window_attention",
    )(sink, q, kt, kt, kt, kt, v2, v2, v2, v2)


def _attention_context(q, kt, v2, sink, *, j, n_latent, ctx_len):
    _, s, _ = q.shape
    assert ctx_len % BLOCK == 0
    body = functools.partial(_attn_body, n_qb=ctx_len // BLOCK, window=False, seq=s, sink_row=j)
    return pl.pallas_call(
        body,
        grid=(1, n_latent),
        in_specs=[pl.BlockSpec(memory_space=pltpu.SMEM),
                  pl.BlockSpec((None, ctx_len, Q_DIM), lambda _, b: (n_latent, b, 0)),
                  pl.BlockSpec((None, KV_DIM, ctx_len), lambda _, b: (n_latent, 0, b)),
                  pl.BlockSpec((None, ctx_len, 2 * KV_DIM), lambda _, b: (n_latent, b, 0))],
        out_specs=pl.BlockSpec((None, ctx_len, Q_DIM), lambda _, b: (0, b, 0)),
        out_shape=jax.ShapeDtypeStruct((1, s, Q_DIM), BF16),
        compiler_params=_params(2),
        name="context_attention",
    )(sink, q, kt, v2)


def _rope_tables(seq):
    rows = seq // GRID_W
    row = jnp.broadcast_to(jnp.arange(rows)[:, None], (rows, GRID_W)).reshape(-1).astype(F32)
    col = jnp.broadcast_to(jnp.arange(GRID_W)[None, :], (rows, GRID_W)).reshape(-1).astype(F32)
    inv_freq = ROPE_THETA ** (-jnp.arange(ROPE_QUARTER, dtype=F32) / ROPE_QUARTER)
    ang_r, ang_c = row[:, None] * inv_freq, col[:, None] * inv_freq
    cos = jnp.concatenate([jnp.cos(ang_r)] * 2 + [jnp.cos(ang_c)] * 2, axis=1)
    sin = jnp.concatenate([-jnp.sin(ang_r), jnp.sin(ang_r), -jnp.sin(ang_c), jnp.sin(ang_c)], axis=1)
    reps = LANES // HEAD_DIM
    cos, sin = jnp.tile(cos, (1, reps)), jnp.tile(sin, (1, reps))
    return jnp.stack([cos, jnp.ones_like(cos)]), jnp.stack([sin, jnp.zeros_like(sin)])


def kernel(x, c, ctx, c_ctx, w_mod, b_mod, norm_g, ffn_w_gu, ffn_w_down, conv_w_in, conv_w, conv_w_out,
           attn_w_qkv, attn_w_o, attn_sink):
    n_latent, seq, d = x.shape
    ctx_len = ctx.shape[1]
    assert d == D_MODEL and n_latent * ctx_len == seq and seq % GRID_W == 0
    assert ctx_len & (ctx_len - 1) == 0 and seq & (seq - 1) == 0
    n_slots = n_latent + 1

    cs = jnp.concatenate([c, c_ctx[None, :], jnp.zeros((MOD_PAD_ROWS - n_slots, d), F32)], axis=0)
    mod_all = _modulation(cs, w_mod, b_mod)[:, :n_slots].reshape(DEPTH, n_slots, N_MOD, d)
    cos, sin = _rope_tables(seq)

    wgu, wd = ffn_w_gu[0, 0].astype(BF16), ffn_w_down[0, 0].astype(BF16)
    xz, ctx_slot = x, ctx.reshape(1, seq, d)
    for i in range(DEPTH):
        last = i == DEPTH - 1
        use_attn = (i % N_MIXERS) == 1
        j = i // N_MIXERS
        live = n_latent if last else n_slots

        mixer_w = ([(attn_w_qkv, (j,)), (attn_w_o, (j,))] if use_attn
                   else [(conv_w_in, (j,)), (conv_w_out, (j,))])
        if i == 0:
            mix_in, mix_out = (w[lead].astype(BF16) for w, lead in mixer_w)
            mixer_w = []
        xz, cast_w = _ffn(
            (xz, ctx_slot), mod_all, norm_g, wgu, wd, [(ffn_w_gu, (i, 1)), (ffn_w_down, (i, 1))] + mixer_w,
            layer=i, half=0, n_slots=n_slots if (use_attn or not last) else n_latent, n_latent=n_latent,
            tm=FFN_ROWS, groups=FFN_ROW_GROUPS)
        wgu, wd = cast_w[:2]
        if mixer_w:
            mix_in, mix_out = cast_w[2:]
        ctx_slot = None
        attn = None
        if use_attn:
            q, kt, v2 = _qkv(xz, mod_all, norm_g, mix_in, cos, sin, layer=i, n_latent=n_latent, tm=MIX_ROWS,
                             q_scale=HEAD_DIM ** -0.5 * LOG2_E)
            o = _attention_latent(q, kt, v2, attn_sink, j=j, n_latent=n_latent, ctx_len=ctx_len, tq=ATTN_ROWS)
            oz = None if last else _attention_context(q, kt, v2, attn_sink, j=j, n_latent=n_latent,
                                                      ctx_len=ctx_len)
            attn = ((o, oz), mix_out)
        else:
            xz = _conv(xz, mod_all, norm_g, mix_in, conv_w, mix_out, layer=i, j=j, n_slots=live,
                       n_latent=n_latent, ctx_len=ctx_len, tm=MIX_ROWS)
        next_ffn = [] if last else [(ffn_w_gu, (i + 1, 0)), (ffn_w_down, (i + 1, 0))]
        xz, next_w = _ffn((xz, None), mod_all, norm_g, wgu, wd, next_ffn, layer=i, half=1, n_slots=live,
                          n_latent=n_latent, tm=FFN_ROWS, groups=FFN_ROW_GROUPS, attn=attn)
        if not last:
            wgu, wd = next_w
    return xz
```

```python
import functools

import jax
import jax.numpy as jnp
from jax import lax
from jax.experimental import pallas as pl
from jax.experimental.pallas import tpu as pltpu

D_MODEL = 1024
DEPTH = 4
GRID_W = 64
N_MIXERS = 2
N_HEADS = 16
N_KV_HEADS = 4
HEAD_DIM = D_MODEL // N_HEADS
GROUP = N_HEADS // N_KV_HEADS
Q_DIM = N_HEADS * HEAD_DIM
KV_DIM = N_KV_HEADS * HEAD_DIM
WINDOW = 128
BLOCK = 128
ROPE_THETA = 10000.0
ROPE_QUARTER = HEAD_DIM // 4
CONV_WIDTH = 3
D_FF = 2816
N_MOD = 9
RMS_EPS = 1e-6
NEG_INF = -1e30
HALF_STEP = 0.5
LOG2_E = 1.4426950408889634

LANES = 128
BF16_SUBLANES = 16
VMEM_LIMIT_BYTES = 56 * 1024 * 1024

FF_CHUNK = 256
CONV_CHUNK = 256
MOD_PAD_ROWS = 16
FFN_ROWS = 1024
FFN_ROW_GROUPS = 4
FFN_FINISH_AT = 2
FFN_START_NEXT_AT = 5
MIX_ROWS = 1024
QKV_ROW_GROUPS = 4
ATTN_ROWS = 8 * BLOCK
ATTN_LOOKAHEAD = 1
CAST_STEPS = 16

F32 = jnp.float32
BF16 = jnp.bfloat16

assert 2 * HEAD_DIM == LANES and WINDOW == BLOCK == LANES


def _rms(x, g):
    return x * lax.rsqrt(jnp.mean(x * x, axis=-1, keepdims=True) + RMS_EPS) * g


def _params(n_grid_dims):
    return pltpu.CompilerParams(dimension_semantics=("arbitrary",) * n_grid_dims,
                                vmem_limit_bytes=VMEM_LIMIT_BYTES)


def _resident(arr, *lead):
    rest = arr.shape[len(lead):]
    index = tuple(lead) + (0,) * len(rest)
    return pl.BlockSpec((None,) * len(lead) + rest, lambda *_: index, pipeline_mode=pl.Buffered(1))


def _mod_spec(mod_all, layer):
    return pl.BlockSpec((None, None) + mod_all.shape[2:], lambda b, t: (layer, b, 0, 0))


def _mod_body(c_ref, w_ref, b_ref, o_ref):
    c = c_ref[...]
    a = (c * jax.nn.sigmoid(c)).astype(BF16)
    o_ref[...] = jnp.dot(a, w_ref[...].astype(BF16), preferred_element_type=F32) + b_ref[...]


def _modulation(cs, w_mod, b_mod):
    depth, d, n = w_mod.shape
    tn = n // 4
    return pl.pallas_call(
        _mod_body,
        grid=(depth, n // tn),
        in_specs=[pl.BlockSpec((MOD_PAD_ROWS, d), lambda l, j: (0, 0)),
                  pl.BlockSpec((None, d, tn), lambda l, j: (l, 0, j)),
                  pl.BlockSpec((None, 1, tn), lambda l, j: (l, 0, j))],
        out_specs=pl.BlockSpec((None, MOD_PAD_ROWS, tn), lambda l, j: (l, 0, j)),
        out_shape=jax.ShapeDtypeStruct((depth, MOD_PAD_ROWS, n), F32),
        compiler_params=_params(2),
        name="modulation",
    )(cs, w_mod, b_mod.reshape(depth, 1, n))


def _slot_stream(lat, ctx, tm, n_latent):
    width = lat.shape[-1]
    if ctx is None:
        return [pl.BlockSpec((None, tm, width), lambda b, t: (b, t, 0))], [lat]
    return ([pl.BlockSpec((None, tm, width), lambda b, t: (jnp.minimum(b, n_latent - 1), t, 0)),
             pl.BlockSpec((None, tm, width), lambda b, t: (0, jnp.where(b == n_latent, t, 0), 0),
                          pipeline_mode=pl.Buffered(1))], [lat, ctx])


def _ffn_body(*refs, k, gi, groups, n_latent, n_x, n_a, n_casts):
    refs = list(refs)
    x_refs, refs = refs[:n_x], refs[n_x:]
    a_refs, refs = refs[:n_a], refs[n_a:]
    wo_ref = refs.pop(0) if n_a else None
    mod_ref, g_ref, wgu_ref, wd_ref = refs[:4]
    cast_src, refs = refs[4:4 + n_casts], refs[4 + n_casts:]
    o_ref = refs.pop(0)
    cast_dst, (h_ref, act_ref) = refs[:n_casts], refs[n_casts:]
    for src, dst in zip(cast_src, cast_dst):
        dst[...] = src[...].astype(BF16)
    in_ctx_slot = pl.program_id(0) == n_latent

    def pick(stream, rs):
        lat = stream[0][rs, :]
        return lat if len(stream) == 1 else jnp.where(in_ctx_slot, stream[1][rs, :], lat)

    shift, scale, gate = mod_ref[k:k + 1, :], mod_ref[k + 1:k + 2, :], mod_ref[k + 2:k + 3, :]
    rows = o_ref.shape[0] // groups

    def finish(rs, x):
        y = jnp.dot(act_ref[rs, :], wd_ref[...], preferred_element_type=F32)
        o_ref[rs, :] = x + HALF_STEP * gate * _rms(y, g_ref[gi + 1:gi + 2, :])

    def start(r):
        rs = slice(r * rows, (r + 1) * rows)
        x = pick(x_refs, rs)
        if n_a:
            mixed = jnp.dot(pick(a_refs, rs), wo_ref[...], preferred_element_type=F32)
            x = x + mod_ref[5:6, :] * _rms(mixed, g_ref[3:4, :])
        h_ref[rs, :] = (_rms(x, g_ref[gi:gi + 1, :]) * (1.0 + scale) + shift).astype(BF16)
        return rs, x

    skew_a, skew_b = FFN_FINISH_AT, FFN_START_NEXT_AT
    pending = None
    cur = start(0)
    for r in range(groups):
        rs, x = cur
        nxt = None
        for c in range(D_FF // FF_CHUNK):
            cs = slice(c * FF_CHUNK, (c + 1) * FF_CHUNK)
            us = slice(D_FF + c * FF_CHUNK, D_FF + (c + 1) * FF_CHUNK)
            gt = jnp.dot(h_ref[rs, :], wgu_ref[:, cs], preferred_element_type=F32)
            up = jnp.dot(h_ref[rs, :], wgu_ref[:, us], preferred_element_type=F32)
            act_ref[rs, cs] = (gt * jax.nn.sigmoid(gt) * up).astype(BF16)
            if c == skew_a and pending is not None:
                finish(*pending)
            if c == skew_b and r + 1 < groups:
                nxt = start(r + 1)
        pending = cur
        if r + 1 < groups:
            cur = nxt if nxt is not None else start(r + 1)
    finish(*pending)


def _ffn(x, mod_all, norm_g, wgu, wd, casts, *, layer, half, n_slots, n_latent, tm, groups, attn=None):
    _, s, d = x[0].shape
    steps_per_slot = s // tm
    assert n_slots * steps_per_slot >= CAST_STEPS
    x_specs, x_ops = _slot_stream(*x, tm, n_latent)
    a_specs, a_ops = [], []
    if attn is not None:
        a_specs, a_ops = _slot_stream(*attn[0], tm, n_latent)
        a_specs, a_ops = a_specs + [_resident(attn[1])], a_ops + [attn[1]]
    body = functools.partial(_ffn_body, k=6 * half, gi=4 * half, groups=groups, n_latent=n_latent,
                             n_x=len(x_ops), n_a=max(len(a_ops) - 1, 0), n_casts=len(casts))

    def cast_block(b, t):
        return jnp.minimum(b * steps_per_slot + t, CAST_STEPS - 1)

    cast_in, cast_out, cast_shapes = [], [], []
    for src, lead in casts:
        r, c = src.shape[len(lead):]
        rows = r // CAST_STEPS
        assert r % CAST_STEPS == 0 and rows % BF16_SUBLANES == 0
        cast_in.append(pl.BlockSpec((None,) * len(lead) + (rows, c),
                                    lambda b, t, lead=tuple(lead): lead + (cast_block(b, t), 0)))
        cast_out.append(pl.BlockSpec((rows, c), lambda b, t: (cast_block(b, t), 0)))
        cast_shapes.append(jax.ShapeDtypeStruct((r, c), BF16))
    outs = pl.pallas_call(
        body,
        grid=(n_slots, steps_per_slot),
        in_specs=x_specs + a_specs + [_mod_spec(mod_all, layer), _resident(norm_g, layer), _resident(wgu),
                                      _resident(wd)] + cast_in,
        out_specs=[pl.BlockSpec((None, tm, d), lambda b, t: (b, t, 0))] + cast_out,
        out_shape=[jax.ShapeDtypeStruct((n_slots, s, d), F32)] + cast_shapes,
        scratch_shapes=[pltpu.VMEM((tm, d), BF16), pltpu.VMEM((tm, D_FF), BF16)],
        compiler_params=_params(2),
        name="ffn",
    )(*x_ops, *a_ops, mod_all, norm_g, wgu, wd, *[src for src, _ in casts])
    return outs[0], outs[1:]


def _conv_body(x_ref, xp_ref, xn_ref, mod_ref, g_ref, win_ref, cw_ref, wout_ref, o_ref,
               h_ref, gy_ref, *, tm, n_latent, seq, ctx_len):
    halo = BF16_SUBLANES
    rows = tm + 2 * halo
    slot, t = pl.program_id(0), pl.program_id(1)
    shift, scale, gate = mod_ref[3:4, :], mod_ref[4:5, :], mod_ref[5:6, :]
    g_in = g_ref[2:3, :]

    def modulated(v):
        return (_rms(v, g_in) * (1.0 + scale) + shift).astype(BF16)

    x = x_ref[...]
    h_ref[0:tm, :] = modulated(x)
    h_ref[tm:tm + halo, :] = modulated(xn_ref[...])
    h_ref[tm + halo:rows, :] = modulated(xp_ref[...])

    period = jnp.where(slot == n_latent, ctx_len, seq)
    pos = (t * tm + lax.broadcasted_iota(jnp.int32, (tm, 1), 0)) & (period - 1)
    has_prev = pos != 0
    has_next = pos != period - 1

    def project(c):
        cols = [slice(part * D_MODEL + c * CONV_CHUNK, part * D_MODEL + (c + 1) * CONV_CHUNK) for part in range(3)]
        return (jnp.dot(h_ref[0:tm, :], win_ref[:, cols[0]], preferred_element_type=F32),
                jnp.dot(h_ref[...], win_ref[:, cols[1]], preferred_element_type=F32),
                jnp.dot(h_ref[...], win_ref[:, cols[2]], preferred_element_type=F32))

    def mix(c, bg, cg, u):
        cs = slice(c * CONV_CHUNK, (c + 1) * CONV_CHUNK)
        cu = cg * u
        prev = jnp.where(has_prev, pltpu.roll(cu, 1, 0)[0:tm, :], 0.0)
        nxt = jnp.where(has_next, pltpu.roll(cu, rows - 1, 0)[0:tm, :], 0.0)
        y = cw_ref[0:1, cs] * prev + cw_ref[1:2, cs] * cu[0:tm, :] + cw_ref[2:3, cs] * nxt
        gy_ref[:, cs] = (bg * y).astype(BF16)

    n_chunks = D_MODEL // CONV_CHUNK
    ready = project(0)
    for c in range(n_chunks):
        ahead = project(c + 1) if c + 1 < n_chunks else None
        mix(c, *ready)
        ready = ahead
    y = jnp.dot(gy_ref[...], wout_ref[...], preferred_element_type=F32)
    o_ref[...] = x + gate * _rms(y, g_ref[3:4, :])


def _conv(xz, mod_all, norm_g, win, cw, wout, *, layer, j, n_slots, n_latent, ctx_len, tm):
    _, s, d = xz.shape
    halo = BF16_SUBLANES
    hb = tm // halo
    last = s // halo - 1
    body = functools.partial(_conv_body, tm=tm, n_latent=n_latent, seq=s, ctx_len=ctx_len)
    return pl.pallas_call(
        body,
        grid=(n_slots, s // tm),
        in_specs=[pl.BlockSpec((None, tm, d), lambda b, t: (b, t, 0)),
                  pl.BlockSpec((None, halo, d), lambda b, t: (b, jnp.maximum(t * hb - 1, 0), 0)),
                  pl.BlockSpec((None, halo, d), lambda b, t: (b, jnp.minimum((t + 1) * hb, last), 0)),
                  _mod_spec(mod_all, layer), _resident(norm_g, layer), _resident(win),
                  _resident(cw, j), _resident(wout)],
        out_specs=pl.BlockSpec((None, tm, d), lambda b, t: (b, t, 0)),
        out_shape=jax.ShapeDtypeStruct((n_slots, s, d), F32),
        scratch_shapes=[pltpu.VMEM((tm + 2 * halo, d), BF16), pltpu.VMEM((tm, d), BF16)],
        compiler_params=_params(2),
        name="conv_mixer",
    )(xz, xz, xz, mod_all, norm_g, win, cw, wout)


def _qkv_body(x_ref, mod_ref, g_ref, w_ref, cos_ref, sin_ref, q_ref, kt_ref, v_ref, h_ref, *, q_scale):
    shift, scale = mod_ref[3:4, :], mod_ref[4:5, :]
    first_half = (lax.broadcasted_iota(jnp.int32, (1, LANES), 1) & (2 * ROPE_QUARTER - 1)) < ROPE_QUARTER
    grp = x_ref.shape[0] // QKV_ROW_GROUPS
    groups = [slice(r * grp, (r + 1) * grp) for r in range(QKV_ROW_GROUPS)]

    def project(rs):
        h_ref[rs, :] = (_rms(x_ref[rs, :], g_ref[2:3, :]) * (1.0 + scale) + shift).astype(BF16)
        return (jnp.dot(h_ref[rs, :], w_ref[:, :Q_DIM + KV_DIM], preferred_element_type=F32),
                jnp.dot(h_ref[rs, :], w_ref[:, Q_DIM + KV_DIM:], preferred_element_type=F32))

    def emit(rs, qk, v):
        cos, sin = cos_ref[rs, :], sin_ref[rs, :]

        def rope(t):
            partner = jnp.where(first_half, pltpu.roll(t, LANES - ROPE_QUARTER, 1),
                                pltpu.roll(t, ROPE_QUARTER, 1))
            return t * cos + partner * sin

        for j in range(Q_DIM // LANES):
            q_ref[rs, j * LANES:(j + 1) * LANES] = (
                rope(qk[:, j * LANES:(j + 1) * LANES]) * q_scale).astype(BF16)
        for j in range(KV_DIM // LANES):
            kt_ref[j * LANES:(j + 1) * LANES, rs] = rope(
                qk[:, Q_DIM + j * LANES:Q_DIM + (j + 1) * LANES]).T.astype(BF16)
        heads = [v[:, h * HEAD_DIM:(h + 1) * HEAD_DIM] for h in range(N_KV_HEADS)]
        v_ref[rs, :] = jnp.concatenate([vh for vh in heads for _ in range(2)], axis=1).astype(BF16)

    ready = project(groups[0])
    for r, rs in enumerate(groups):
        ahead = project(groups[r + 1]) if r + 1 < len(groups) else None
        emit(rs, *ready)
        ready = ahead


def _qkv(xz, mod_all, norm_g, w, cos, sin, *, layer, n_latent, tm, q_scale):
    n_slots, s, d = xz.shape
    tok = lambda b, t: (b, t, 0)
    tab = lambda b, t: (jnp.where(b == n_latent, 1, 0), t, 0)
    return pl.pallas_call(
        functools.partial(_qkv_body, q_scale=q_scale),
        grid=(n_slots, s // tm),
        in_specs=[pl.BlockSpec((None, tm, d), tok),
                  _mod_spec(mod_all, layer), _resident(norm_g, layer), _resident(w),
                  pl.BlockSpec((None, tm, LANES), tab), pl.BlockSpec((None, tm, LANES), tab)],
        out_specs=[pl.BlockSpec((None, tm, Q_DIM), tok),
                   pl.BlockSpec((None, KV_DIM, tm), lambda b, t: (b, 0, t)),
                   pl.BlockSpec((None, tm, 2 * KV_DIM), tok)],
        out_shape=[jax.ShapeDtypeStruct((n_slots, s, Q_DIM), BF16),
                   jax.ShapeDtypeStruct((n_slots, KV_DIM, s), BF16),
                   jax.ShapeDtypeStruct((n_slots, s, 2 * KV_DIM), BF16)],
        scratch_shapes=[pltpu.VMEM((tm, d), BF16)],
        compiler_params=_params(2),
        name="qkv_rope",
    )(xz, mod_all, norm_g, w, cos, sin)


def _attn_body(sink_ref, q_ref, *refs, n_qb, window, seq, sink_row):
    if window:
        ktp_ref, ktc_ref, ktn_ref, ktz_ref, vp_ref, vc_ref, vn_ref, vz_ref, o_ref = refs
        kt_all = jnp.concatenate([ktp_ref[...], ktc_ref[...], ktn_ref[...]], axis=1)
        v_all = jnp.concatenate([vp_ref[...], vc_ref[...], vn_ref[...]], axis=0)
    else:
        ktz_ref, vz_ref, o_ref = refs
    t = pl.program_id(1)
    row = lax.broadcasted_iota(jnp.int32, (BLOCK, BLOCK), 0)
    col = lax.broadcasted_iota(jnp.int32, (BLOCK, BLOCK), 1)
    low_lanes = lax.broadcasted_iota(jnp.int32, (1, LANES), 1) < HEAD_DIM
    def operands(i, h):
        hr = slice(h * HEAD_DIM, (h + 1) * HEAD_DIM)
        hl = slice(h * LANES, (h + 1) * LANES)
        if window:
            keys = jnp.concatenate([kt_all[hr, i * BLOCK:(i + 3) * BLOCK], ktz_ref[hr, :]], axis=1)
            vals = jnp.concatenate([v_all[i * BLOCK:(i + 3) * BLOCK, hl], vz_ref[:, hl]], axis=0)
        else:
            keys, vals = ktz_ref[hr, :], vz_ref[:, hl]
        zk = jnp.zeros_like(keys)
        rhs = jnp.concatenate([jnp.concatenate([keys, zk], axis=1),
                               jnp.concatenate([zk, keys], axis=1)], axis=0)
        low_v = lax.broadcasted_iota(jnp.int32, vals.shape, 1) < HEAD_DIM
        zv = jnp.zeros_like(vals)
        ones_lo = jnp.where(low_v, 1.0, 0.0).astype(BF16)
        ones_hi = jnp.where(low_v, 0.0, 1.0).astype(BF16)
        vv = jnp.concatenate([jnp.concatenate([jnp.where(low_v, vals, zv), ones_lo], axis=1),
                              jnp.concatenate([jnp.where(low_v, zv, vals), ones_hi], axis=1)], axis=0)
        return rhs, vv

    def scores(i, j, rhs):
        return jnp.dot(q_ref[i * BLOCK:(i + 1) * BLOCK, j * LANES:(j + 1) * LANES], rhs,
                       preferred_element_type=F32)

    def finish(i, j, s, vv):
        nk = s.shape[1] // 2
        if window:
            n = t * n_qb + i
            keep_prev = (col >= row) & (n > 0)
            keep_next = (col <= row) & (n < seq // BLOCK - 1)
        ps, sink_terms = [], []
        for e in range(2):
            sink = sink_ref[sink_row, 2 * j + e] * LOG2_E
            se = s[:, e * nk:(e + 1) * nk]
            if window:
                se = jnp.concatenate(
                    [jnp.where(keep_prev, se[:, :BLOCK], NEG_INF), se[:, BLOCK:2 * BLOCK],
                     jnp.where(keep_next, se[:, 2 * BLOCK:3 * BLOCK], NEG_INF), se[:, 3 * BLOCK:]], axis=1)
            m = jnp.maximum(jnp.max(se, axis=-1, keepdims=True), sink)
            ps.append(jnp.exp2(se - m).astype(BF16))
            sink_terms.append(jnp.exp2(sink - m))
        ov = jnp.dot(jnp.concatenate(ps, axis=1), vv, preferred_element_type=F32)
        denom = ov[:, LANES:] + jnp.where(low_lanes, sink_terms[0], sink_terms[1])
        o_ref[i * BLOCK:(i + 1) * BLOCK, j * LANES:(j + 1) * LANES] = (ov[:, :LANES] / denom).astype(BF16)

    in_flight = []
    for i in range(n_qb):
        for h in range(N_KV_HEADS):
            rhs, vv = operands(i, h)
            for jj in range(GROUP // 2):
                j = (GROUP // 2) * h + jj
                in_flight.append((i, j, scores(i, j, rhs), vv))
                if len(in_flight) > ATTN_LOOKAHEAD:
                    finish(*in_flight.pop(0))
    for item in in_flight:
        finish(*item)


def _attention_latent(q, kt, v2, sink, *, j, n_latent, ctx_len, tq):
    _, s, _ = q.shape
    assert tq % BLOCK == 0 and s % tq == 0
    qpb = tq // BLOCK
    nb = s // BLOCK
    body = functools.partial(_attn_body, n_qb=qpb, window=True, seq=s, sink_row=j)
    kt_spec = lambda w, f: pl.BlockSpec((None, KV_DIM, w), f)
    v_spec = lambda w, f: pl.BlockSpec((None, w, 2 * KV_DIM), f)
    return pl.pallas_call(
        body,
        grid=(n_latent, s // tq),
        in_specs=[pl.BlockSpec(memory_space=pltpu.SMEM),
                  pl.BlockSpec((None, tq, Q_DIM), lambda b, t: (b, t, 0)),
                  kt_spec(BLOCK, lambda b, t: (b, 0, jnp.maximum(t * qpb - 1, 0))),
                  kt_spec(tq, lambda b, t: (b, 0, t)),
                  kt_spec(BLOCK, lambda b, t: (b, 0, jnp.minimum((t + 1) * qpb, nb - 1))),
                  kt_spec(ctx_len, lambda b, t: (n_latent, 0, b)),
                  v_spec(BLOCK, lambda b, t: (b, jnp.maximum(t * qpb - 1, 0), 0)),
                  v_spec(tq, lambda b, t: (b, t, 0)),
                  v_spec(BLOCK, lambda b, t: (b, jnp.minimum((t + 1) * qpb, nb - 1), 0)),
                  v_spec(ctx_len, lambda b, t: (n_latent, b, 0))],
        out_specs=pl.BlockSpec((None, tq, Q_DIM), lambda b, t: (b, t, 0)),
        out_shape=jax.ShapeDtypeStruct((n_latent, s, Q_DIM), BF16),
        compiler_params=_params(2),
        name="window_attention",
    )(sink, q, kt, kt, kt, kt, v2, v2, v2, v2)


def _attention_context(q, kt, v2, sink, *, j, n_latent, ctx_len):
    _, s, _ = q.shape
    assert ctx_len % BLOCK == 0
    body = functools.partial(_attn_body, n_qb=ctx_len // BLOCK, window=False, seq=s, sink_row=j)
    return pl.pallas_call(
        body,
        grid=(1, n_latent),
        in_specs=[pl.BlockSpec(memory_space=pltpu.SMEM),
                  pl.BlockSpec((None, ctx_len, Q_DIM), lambda _, b: (n_latent, b, 0)),
                  pl.BlockSpec((None, KV_DIM, ctx_len), lambda _, b: (n_latent, 0, b)),
                  pl.BlockSpec((None, ctx_len, 2 * KV_DIM), lambda _, b: (n_latent, b, 0))],
        out_specs=pl.BlockSpec((None, ctx_len, Q_DIM), lambda _, b: (0, b, 0)),
        out_shape=jax.ShapeDtypeStruct((1, s, Q_DIM), BF16),
        compiler_params=_params(2),
        name="context_attention",
    )(sink, q, kt, v2)


def _rope_tables(seq):
    rows = seq // GRID_W
    row = jnp.broadcast_to(jnp.arange(rows)[:, None], (rows, GRID_W)).reshape(-1).astype(F32)
    col = jnp.broadcast_to(jnp.arange(GRID_W)[None, :], (rows, GRID_W)).reshape(-1).astype(F32)
    inv_freq = ROPE_THETA ** (-jnp.arange(ROPE_QUARTER, dtype=F32) / ROPE_QUARTER)
    ang_r, ang_c = row[:, None] * inv_freq, col[:, None] * inv_freq
    cos = jnp.concatenate([jnp.cos(ang_r)] * 2 + [jnp.cos(ang_c)] * 2, axis=1)
    sin = jnp.concatenate([-jnp.sin(ang_r), jnp.sin(ang_r), -jnp.sin(ang_c), jnp.sin(ang_c)], axis=1)
    reps = LANES // HEAD_DIM
    cos, sin = jnp.tile(cos, (1, reps)), jnp.tile(sin, (1, reps))
    return jnp.stack([cos, jnp.ones_like(cos)]), jnp.stack([sin, jnp.zeros_like(sin)])


def kernel(x, c, ctx, c_ctx, w_mod, b_mod, norm_g, ffn_w_gu, ffn_w_down, conv_w_in, conv_w, conv_w_out,
           attn_w_qkv, attn_w_o, attn_sink):
    n_latent, seq, d = x.shape
    ctx_len = ctx.shape[1]
    assert d == D_MODEL and n_latent * ctx_len == seq and seq % GRID_W == 0
    assert ctx_len & (ctx_len - 1) == 0 and seq & (seq - 1) == 0
    n_slots = n_latent + 1

    cs = jnp.concatenate([c, c_ctx[None, :], jnp.zeros((MOD_PAD_ROWS - n_slots, d), F32)], axis=0)
    mod_all = _modulation(cs, w_mod, b_mod)[:, :n_slots].reshape(DEPTH, n_slots, N_MOD, d)
    cos, sin = _rope_tables(seq)

    wgu, wd = ffn_w_gu[0, 0].astype(BF16), ffn_w_down[0, 0].astype(BF16)
    xz, ctx_slot = x, ctx.reshape(1, seq, d)
    for i in range(DEPTH):
        last = i == DEPTH - 1
        use_attn = (i % N_MIXERS) == 1
        j = i // N_MIXERS
        live = n_latent if last else n_slots

        mixer_w = ([(attn_w_qkv, (j,)), (attn_w_o, (j,))] if use_attn
                   else [(conv_w_in, (j,)), (conv_w_out, (j,))])
        if i == 0:
            mix_in, mix_out = (w[lead].astype(BF16) for w, lead in mixer_w)
            mixer_w = []
        xz, cast_w = _ffn(
            (xz, ctx_slot), mod_all, norm_g, wgu, wd, [(ffn_w_gu, (i, 1)), (ffn_w_down, (i, 1))] + mixer_w,
            layer=i, half=0, n_slots=n_slots if (use_attn or not last) else n_latent, n_latent=n_latent,
            tm=FFN_ROWS, groups=FFN_ROW_GROUPS)
        wgu, wd = cast_w[:2]
        if mixer_w:
            mix_in, mix_out = cast_w[2:]
        ctx_slot = None
        attn = None
        if use_attn:
            q, kt, v2 = _qkv(xz, mod_all, norm_g, mix_in, cos, sin, layer=i, n_latent=n_latent, tm=MIX_ROWS,
                             q_scale=HEAD_DIM ** -0.5 * LOG2_E)
            o = _attention_latent(q, kt, v2, attn_sink, j=j, n_latent=n_latent, ctx_len=ctx_len, tq=ATTN_ROWS)
            oz = None if last else _attention_context(q, kt, v2, attn_sink, j=j, n_latent=n_latent,
                                                      ctx_len=ctx_len)
            attn = ((o, oz), mix_out)
        else:
            xz = _conv(xz, mod_all, norm_g, mix_in, conv_w, mix_out, layer=i, j=j, n_slots=live,
                       n_latent=n_latent, ctx_len=ctx_len, tm=MIX_ROWS)
        next_ffn = [] if last else [(ffn_w_gu, (i + 1, 0)), (ffn_w_down, (i + 1, 0))]
        xz, next_w = _ffn((xz, None), mod_all, norm_g, wgu, wd, next_ffn, layer=i, half=1, n_slots=live,
                          n_latent=n_latent, tm=FFN_ROWS, groups=FFN_ROW_GROUPS, attn=attn)
        if not last:
            wgu, wd = next_w
    return xz
```

```python
import functools

import jax
import jax.numpy as jnp
from jax import lax
from jax.experimental import pallas as pl
from jax.experimental.pallas import tpu as pltpu

D_MODEL = 1024
DEPTH = 4
GRID_W = 64
N_MIXERS = 2
N_HEADS = 16
N_KV_HEADS = 4
HEAD_DIM = D_MODEL // N_HEADS
GROUP = N_HEADS // N_KV_HEADS
Q_DIM = N_HEADS * HEAD_DIM
KV_DIM = N_KV_HEADS * HEAD_DIM
WINDOW = 128
BLOCK = 128
ROPE_THETA = 10000.0
ROPE_QUARTER = HEAD_DIM // 4
CONV_WIDTH = 3
D_FF = 2816
N_MOD = 9
RMS_EPS = 1e-6
NEG_INF = -1e30
HALF_STEP = 0.5
LOG2_E = 1.4426950408889634

LANES = 128
BF16_SUBLANES = 16
VMEM_LIMIT_BYTES = 56 * 1024 * 1024

FF_CHUNK = 256
CONV_CHUNK = 256
MOD_PAD_ROWS = 16
FFN_ROWS = 1024
FFN_ROW_GROUPS = 2
FUSED_FFN_ROW_GROUPS = 4
FFN_FINISH_AT = 2
FFN_START_NEXT_AT = 5
MIX_ROWS = 1024
QKV_ROW_GROUPS = 4
ATTN_ROWS = 8 * BLOCK
ATTN_LOOKAHEAD = 1
CAST_STEPS = 16

F32 = jnp.float32
BF16 = jnp.bfloat16

assert 2 * HEAD_DIM == LANES and WINDOW == BLOCK == LANES


def _rms(x, g):
    return x * lax.rsqrt(jnp.mean(x * x, axis=-1, keepdims=True) + RMS_EPS) * g


def _params(n_grid_dims):
    return pltpu.CompilerParams(dimension_semantics=("arbitrary",) * n_grid_dims,
                                vmem_limit_bytes=VMEM_LIMIT_BYTES)


def _resident(arr, *lead):
    rest = arr.shape[len(lead):]
    index = tuple(lead) + (0,) * len(rest)
    return pl.BlockSpec((None,) * len(lead) + rest, lambda *_: index, pipeline_mode=pl.Buffered(1))


def _mod_spec(mod_all, layer):
    return pl.BlockSpec((None, None) + mod_all.shape[2:], lambda b, t: (layer, b, 0, 0))


def _mod_body(c_ref, w_ref, b_ref, o_ref):
    c = c_ref[...]
    a = (c * jax.nn.sigmoid(c)).astype(BF16)
    o_ref[...] = jnp.dot(a, w_ref[...].astype(BF16), preferred_element_type=F32) + b_ref[...]


def _modulation(cs, w_mod, b_mod):
    depth, d, n = w_mod.shape
    tn = n // 4
    return pl.pallas_call(
        _mod_body,
        grid=(depth, n // tn),
        in_specs=[pl.BlockSpec((MOD_PAD_ROWS, d), lambda l, j: (0, 0)),
                  pl.BlockSpec((None, d, tn), lambda l, j: (l, 0, j)),
                  pl.BlockSpec((None, 1, tn), lambda l, j: (l, 0, j))],
        out_specs=pl.BlockSpec((None, MOD_PAD_ROWS, tn), lambda l, j: (l, 0, j)),
        out_shape=jax.ShapeDtypeStruct((depth, MOD_PAD_ROWS, n), F32),
        compiler_params=_params(2),
        name="modulation",
    )(cs, w_mod, b_mod.reshape(depth, 1, n))


def _slot_stream(lat, ctx, tm, n_latent):
    width = lat.shape[-1]
    if ctx is None:
        return [pl.BlockSpec((None, tm, width), lambda b, t: (b, t, 0))], [lat]
    return ([pl.BlockSpec((None, tm, width), lambda b, t: (jnp.minimum(b, n_latent - 1), t, 0)),
             pl.BlockSpec((None, tm, width), lambda b, t: (0, jnp.where(b == n_latent, t, 0), 0),
                          pipeline_mode=pl.Buffered(1))], [lat, ctx])


def _ffn_body(*refs, k, gi, groups, n_latent, n_x, n_a, n_casts):
    refs = list(refs)
    x_refs, refs = refs[:n_x], refs[n_x:]
    a_refs, refs = refs[:n_a], refs[n_a:]
    wo_ref = refs.pop(0) if n_a else None
    mod_ref, g_ref, wgu_ref, wd_ref = refs[:4]
    cast_src, refs = refs[4:4 + n_casts], refs[4 + n_casts:]
    o_ref = refs.pop(0)
    cast_dst, (h_ref, act_ref) = refs[:n_casts], refs[n_casts:]
    for src, dst in zip(cast_src, cast_dst):
        dst[...] = src[...].astype(BF16)
    in_ctx_slot = pl.program_id(0) == n_latent

    def pick(stream, rs):
        lat = stream[0][rs, :]
        return lat if len(stream) == 1 else jnp.where(in_ctx_slot, stream[1][rs, :], lat)

    shift, scale, gate = mod_ref[k:k + 1, :], mod_ref[k + 1:k + 2, :], mod_ref[k + 2:k + 3, :]
    rows = o_ref.shape[0] // groups

    def finish(rs, x):
        y = jnp.dot(act_ref[rs, :], wd_ref[...], preferred_element_type=F32)
        o_ref[rs, :] = x + HALF_STEP * gate * _rms(y, g_ref[gi + 1:gi + 2, :])

    def start(r):
        rs = slice(r * rows, (r + 1) * rows)
        x = pick(x_refs, rs)
        if n_a:
            mixed = jnp.dot(pick(a_refs, rs), wo_ref[...], preferred_element_type=F32)
            x = x + mod_ref[5:6, :] * _rms(mixed, g_ref[3:4, :])
        h_ref[rs, :] = (_rms(x, g_ref[gi:gi + 1, :]) * (1.0 + scale) + shift).astype(BF16)
        return rs, x

    skew_a, skew_b = FFN_FINISH_AT, FFN_START_NEXT_AT
    pending = None
    cur = start(0)
    for r in range(groups):
        rs, x = cur
        nxt = None
        for c in range(D_FF // FF_CHUNK):
            cs = slice(c * FF_CHUNK, (c + 1) * FF_CHUNK)
            us = slice(D_FF + c * FF_CHUNK, D_FF + (c + 1) * FF_CHUNK)
            gt = jnp.dot(h_ref[rs, :], wgu_ref[:, cs], preferred_element_type=F32)
            up = jnp.dot(h_ref[rs, :], wgu_ref[:, us], preferred_element_type=F32)
            act_ref[rs, cs] = (gt * jax.nn.sigmoid(gt) * up).astype(BF16)
            if c == skew_a and pending is not None:
                finish(*pending)
            if c == skew_b and r + 1 < groups:
                nxt = start(r + 1)
        pending = cur
        if r + 1 < groups:
            cur = nxt if nxt is not None else start(r + 1)
    finish(*pending)


def _ffn(x, mod_all, norm_g, wgu, wd, casts, *, layer, half, n_slots, n_latent, tm, groups, attn=None):
    _, s, d = x[0].shape
    steps_per_slot = s // tm
    assert n_slots * steps_per_slot >= CAST_STEPS
    x_specs, x_ops = _slot_stream(*x, tm, n_latent)
    a_specs, a_ops = [], []
    if attn is not None:
        a_specs, a_ops = _slot_stream(*attn[0], tm, n_latent)
        a_specs, a_ops = a_specs + [_resident(attn[1])], a_ops + [attn[1]]
    body = functools.partial(_ffn_body, k=6 * half, gi=4 * half, groups=groups, n_latent=n_latent,
                             n_x=len(x_ops), n_a=max(len(a_ops) - 1, 0), n_casts=len(casts))

    def cast_block(b, t):
        return jnp.minimum(b * steps_per_slot + t, CAST_STEPS - 1)

    cast_in, cast_out, cast_shapes = [], [], []
    for src, lead in casts:
        r, c = src.shape[len(lead):]
        rows = r // CAST_STEPS
        assert r % CAST_STEPS == 0 and rows % BF16_SUBLANES == 0
        cast_in.append(pl.BlockSpec((None,) * len(lead) + (rows, c),
                                    lambda b, t, lead=tuple(lead): lead + (cast_block(b, t), 0)))
        cast_out.append(pl.BlockSpec((rows, c), lambda b, t: (cast_block(b, t), 0)))
        cast_shapes.append(jax.ShapeDtypeStruct((r, c), BF16))
    outs = pl.pallas_call(
        body,
        grid=(n_slots, steps_per_slot),
        in_specs=x_specs + a_specs + [_mod_spec(mod_all, layer), _resident(norm_g, layer), _resident(wgu),
                                      _resident(wd)] + cast_in,
        out_specs=[pl.BlockSpec((None, tm, d), lambda b, t: (b, t, 0))] + cast_out,
        out_shape=[jax.ShapeDtypeStruct((n_slots, s, d), F32)] + cast_shapes,
        scratch_shapes=[pltpu.VMEM((tm, d), BF16), pltpu.VMEM((tm, D_FF), BF16)],
        compiler_params=_params(2),
        name="ffn",
    )(*x_ops, *a_ops, mod_all, norm_g, wgu, wd, *[src for src, _ in casts])
    return outs[0], outs[1:]


def _conv_body(x_ref, xp_ref, xn_ref, mod_ref, g_ref, win_ref, cw_ref, wout_ref, o_ref,
               h_ref, gy_ref, *, tm, n_latent, seq, ctx_len):
    halo = BF16_SUBLANES
    rows = tm + 2 * halo
    slot, t = pl.program_id(0), pl.program_id(1)
    shift, scale, gate = mod_ref[3:4, :], mod_ref[4:5, :], mod_ref[5:6, :]
    g_in = g_ref[2:3, :]

    def modulated(v):
        return (_rms(v, g_in) * (1.0 + scale) + shift).astype(BF16)

    x = x_ref[...]
    h_ref[0:tm, :] = modulated(x)
    h_ref[tm:tm + halo, :] = modulated(xn_ref[...])
    h_ref[tm + halo:rows, :] = modulated(xp_ref[...])

    period = jnp.where(slot == n_latent, ctx_len, seq)
    pos = (t * tm + lax.broadcasted_iota(jnp.int32, (tm, 1), 0)) & (period - 1)
    has_prev = pos != 0
    has_next = pos != period - 1

    def project(c):
        cols = [slice(part * D_MODEL + c * CONV_CHUNK, part * D_MODEL + (c + 1) * CONV_CHUNK) for part in range(3)]
        return (jnp.dot(h_ref[0:tm, :], win_ref[:, cols[0]], preferred_element_type=F32),
                jnp.dot(h_ref[...], win_ref[:, cols[1]], preferred_element_type=F32),
                jnp.dot(h_ref[...], win_ref[:, cols[2]], preferred_element_type=F32))

    def mix(c, bg, cg, u):
        cs = slice(c * CONV_CHUNK, (c + 1) * CONV_CHUNK)
        cu = cg * u
        prev = jnp.where(has_prev, pltpu.roll(cu, 1, 0)[0:tm, :], 0.0)
        nxt = jnp.where(has_next, pltpu.roll(cu, rows - 1, 0)[0:tm, :], 0.0)
        y = cw_ref[0:1, cs] * prev + cw_ref[1:2, cs] * cu[0:tm, :] + cw_ref[2:3, cs] * nxt
        gy_ref[:, cs] = (bg * y).astype(BF16)

    n_chunks = D_MODEL // CONV_CHUNK
    ready = project(0)
    for c in range(n_chunks):
        ahead = project(c + 1) if c + 1 < n_chunks else None
        mix(c, *ready)
        ready = ahead
    y = jnp.dot(gy_ref[...], wout_ref[...], preferred_element_type=F32)
    o_ref[...] = x + gate * _rms(y, g_ref[3:4, :])


def _conv(xz, mod_all, norm_g, win, cw, wout, *, layer, j, n_slots, n_latent, ctx_len, tm):
    _, s, d = xz.shape
    halo = BF16_SUBLANES
    hb = tm // halo
    last = s // halo - 1
    body = functools.partial(_conv_body, tm=tm, n_latent=n_latent, seq=s, ctx_len=ctx_len)
    return pl.pallas_call(
        body,
        grid=(n_slots, s // tm),
        in_specs=[pl.BlockSpec((None, tm, d), lambda b, t: (b, t, 0)),
                  pl.BlockSpec((None, halo, d), lambda b, t: (b, jnp.maximum(t * hb - 1, 0), 0)),
                  pl.BlockSpec((None, halo, d), lambda b, t: (b, jnp.minimum((t + 1) * hb, last), 0)),
                  _mod_spec(mod_all, layer), _resident(norm_g, layer), _resident(win),
                  _resident(cw, j), _resident(wout)],
        out_specs=pl.BlockSpec((None, tm, d), lambda b, t: (b, t, 0)),
        out_shape=jax.ShapeDtypeStruct((n_slots, s, d), F32),
        scratch_shapes=[pltpu.VMEM((tm + 2 * halo, d), BF16), pltpu.VMEM((tm, d), BF16)],
        compiler_params=_params(2),
        name="conv_mixer",
    )(xz, xz, xz, mod_all, norm_g, win, cw, wout)


def _qkv_body(x_ref, mod_ref, g_ref, w_ref, cos_ref, sin_ref, q_ref, kt_ref, v_ref, h_ref, *, q_scale):
    shift, scale = mod_ref[3:4, :], mod_ref[4:5, :]
    first_half = (lax.broadcasted_iota(jnp.int32, (1, LANES), 1) & (2 * ROPE_QUARTER - 1)) < ROPE_QUARTER
    grp = x_ref.shape[0] // QKV_ROW_GROUPS
    groups = [slice(r * grp, (r + 1) * grp) for r in range(QKV_ROW_GROUPS)]

    def project(rs):
        h_ref[rs, :] = (_rms(x_ref[rs, :], g_ref[2:3, :]) * (1.0 + scale) + shift).astype(BF16)
        return (jnp.dot(h_ref[rs, :], w_ref[:, :Q_DIM + KV_DIM], preferred_element_type=F32),
                jnp.dot(h_ref[rs, :], w_ref[:, Q_DIM + KV_DIM:], preferred_element_type=F32))

    def emit(rs, qk, v):
        cos, sin = cos_ref[rs, :], sin_ref[rs, :]

        def rope(t):
            partner = jnp.where(first_half, pltpu.roll(t, LANES - ROPE_QUARTER, 1),
                                pltpu.roll(t, ROPE_QUARTER, 1))
            return t * cos + partner * sin

        for j in range(Q_DIM // LANES):
            q_ref[rs, j * LANES:(j + 1) * LANES] = (
                rope(qk[:, j * LANES:(j + 1) * LANES]) * q_scale).astype(BF16)
        for j in range(KV_DIM // LANES):
            kt_ref[j * LANES:(j + 1) * LANES, rs] = rope(
                qk[:, Q_DIM + j * LANES:Q_DIM + (j + 1) * LANES]).T.astype(BF16)
        heads = [v[:, h * HEAD_DIM:(h + 1) * HEAD_DIM] for h in range(N_KV_HEADS)]
        v_ref[rs, :] = jnp.concatenate([vh for vh in heads for _ in range(2)], axis=1).astype(BF16)

    ready = project(groups[0])
    for r, rs in enumerate(groups):
        ahead = project(groups[r + 1]) if r + 1 < len(groups) else None
        emit(rs, *ready)
        ready = ahead


def _qkv(xz, mod_all, norm_g, w, cos, sin, *, layer, n_latent, tm, q_scale):
    n_slots, s, d = xz.shape
    tok = lambda b, t: (b, t, 0)
    tab = lambda b, t: (jnp.where(b == n_latent, 1, 0), t, 0)
    return pl.pallas_call(
        functools.partial(_qkv_body, q_scale=q_scale),
        grid=(n_slots, s // tm),
        in_specs=[pl.BlockSpec((None, tm, d), tok),
                  _mod_spec(mod_all, layer), _resident(norm_g, layer), _resident(w),
                  pl.BlockSpec((None, tm, LANES), tab), pl.BlockSpec((None, tm, LANES), tab)],
        out_specs=[pl.BlockSpec((None, tm, Q_DIM), tok),
                   pl.BlockSpec((None, KV_DIM, tm), lambda b, t: (b, 0, t)),
                   pl.BlockSpec((None, tm, 2 * KV_DIM), tok)],
        out_shape=[jax.ShapeDtypeStruct((n_slots, s, Q_DIM), BF16),
                   jax.ShapeDtypeStruct((n_slots, KV_DIM, s), BF16),
                   jax.ShapeDtypeStruct((n_slots, s, 2 * KV_DIM), BF16)],
        scratch_shapes=[pltpu.VMEM((tm, d), BF16)],
        compiler_params=_params(2),
        name="qkv_rope",
    )(xz, mod_all, norm_g, w, cos, sin)


def _attn_body(sink_ref, q_ref, *refs, n_qb, window, seq, sink_row):
    if window:
        ktp_ref, ktc_ref, ktn_ref, ktz_ref, vp_ref, vc_ref, vn_ref, vz_ref, o_ref = refs
        kt_all = jnp.concatenate([ktp_ref[...], ktc_ref[...], ktn_ref[...]], axis=1)
        v_all = jnp.concatenate([vp_ref[...], vc_ref[...], vn_ref[...]], axis=0)
    else:
        ktz_ref, vz_ref, o_ref = refs
    t = pl.program_id(1)
    row = lax.broadcasted_iota(jnp.int32, (BLOCK, BLOCK), 0)
    col = lax.broadcasted_iota(jnp.int32, (BLOCK, BLOCK), 1)
    low_lanes = lax.broadcasted_iota(jnp.int32, (1, LANES), 1) < HEAD_DIM
    def operands(i, h):
        hr = slice(h * HEAD_DIM, (h + 1) * HEAD_DIM)
        hl = slice(h * LANES, (h + 1) * LANES)
        if window:
            keys = jnp.concatenate([kt_all[hr, i * BLOCK:(i + 3) * BLOCK], ktz_ref[hr, :]], axis=1)
            vals = jnp.concatenate([v_all[i * BLOCK:(i + 3) * BLOCK, hl], vz_ref[:, hl]], axis=0)
        else:
            keys, vals = ktz_ref[hr, :], vz_ref[:, hl]
        zk = jnp.zeros_like(keys)
        rhs = jnp.concatenate([jnp.concatenate([keys, zk], axis=1),
                               jnp.concatenate([zk, keys], axis=1)], axis=0)
        low_v = lax.broadcasted_iota(jnp.int32, vals.shape, 1) < HEAD_DIM
        zv = jnp.zeros_like(vals)
        ones_lo = jnp.where(low_v, 1.0, 0.0).astype(BF16)
        ones_hi = jnp.where(low_v, 0.0, 1.0).astype(BF16)
        vv = jnp.concatenate([jnp.concatenate([jnp.where(low_v, vals, zv), ones_lo], axis=1),
                              jnp.concatenate([jnp.where(low_v, zv, vals), ones_hi], axis=1)], axis=0)
        return rhs, vv

    def scores(i, j, rhs):
        return jnp.dot(q_ref[i * BLOCK:(i + 1) * BLOCK, j * LANES:(j + 1) * LANES], rhs,
                       preferred_element_type=F32)

    def finish(i, j, s, vv):
        nk = s.shape[1] // 2
        if window:
            n = t * n_qb + i
            keep_prev = (col >= row) & (n > 0)
            keep_next = (col <= row) & (n < seq // BLOCK - 1)
        ps, sink_terms = [], []
        for e in range(2):
            sink = sink_ref[sink_row, 2 * j + e] * LOG2_E
            se = s[:, e * nk:(e + 1) * nk]
            if window:
                se = jnp.concatenate(
                    [jnp.where(keep_prev, se[:, :BLOCK], NEG_INF), se[:, BLOCK:2 * BLOCK],
                     jnp.where(keep_next, se[:, 2 * BLOCK:3 * BLOCK], NEG_INF), se[:, 3 * BLOCK:]], axis=1)
            m = jnp.maximum(jnp.max(se, axis=-1, keepdims=True), sink)
            ps.append(jnp.exp2(se - m).astype(BF16))
            sink_terms.append(jnp.exp2(sink - m))
        ov = jnp.dot(jnp.concatenate(ps, axis=1), vv, preferred_element_type=F32)
        denom = ov[:, LANES:] + jnp.where(low_lanes, sink_terms[0], sink_terms[1])
        o_ref[i * BLOCK:(i + 1) * BLOCK, j * LANES:(j + 1) * LANES] = (ov[:, :LANES] / denom).astype(BF16)

    in_flight = []
    for i in range(n_qb):
        for h in range(N_KV_HEADS):
            rhs, vv = operands(i, h)
            for jj in range(GROUP // 2):
                j = (GROUP // 2) * h + jj
                in_flight.append((i, j, scores(i, j, rhs), vv))
                if len(in_flight) > ATTN_LOOKAHEAD:
                    finish(*in_flight.pop(0))
    for item in in_flight:
        finish(*item)


def _attention_latent(q, kt, v2, sink, *, j, n_latent, ctx_len, tq):
    _, s, _ = q.shape
    assert tq % BLOCK == 0 and s % tq == 0
    qpb = tq // BLOCK
    nb = s // BLOCK
    body = functools.partial(_attn_body, n_qb=qpb, window=True, seq=s, sink_row=j)
    kt_spec = lambda w, f: pl.BlockSpec((None, KV_DIM, w), f)
    v_spec = lambda w, f: pl.BlockSpec((None, w, 2 * KV_DIM), f)
    return pl.pallas_call(
        body,
        grid=(n_latent, s // tq),
        in_specs=[pl.BlockSpec(memory_space=pltpu.SMEM),
                  pl.BlockSpec((None, tq, Q_DIM), lambda b, t: (b, t, 0)),
                  kt_spec(BLOCK, lambda b, t: (b, 0, jnp.maximum(t * qpb - 1, 0))),
                  kt_spec(tq, lambda b, t: (b, 0, t)),
                  kt_spec(BLOCK, lambda b, t: (b, 0, jnp.minimum((t + 1) * qpb, nb - 1))),
                  kt_spec(ctx_len, lambda b, t: (n_latent, 0, b)),
                  v_spec(BLOCK, lambda b, t: (b, jnp.maximum(t * qpb - 1, 0), 0)),
                  v_spec(tq, lambda b, t: (b, t, 0)),
                  v_spec(BLOCK, lambda b, t: (b, jnp.minimum((t + 1) * qpb, nb - 1), 0)),
                  v_spec(ctx_len, lambda b, t: (n_latent, b, 0))],
        out_specs=pl.BlockSpec((None, tq, Q_DIM), lambda b, t: (b, t, 0)),
        out_shape=jax.ShapeDtypeStruct((n_latent, s, Q_DIM), BF16),
        compiler_params=_params(2),
        name="window_attention",
    )(sink, q, kt, kt, kt, kt, v2, v2, v2, v2)


def _attention_context(q, kt, v2, sink, *, j, n_latent, ctx_len):
    _, s, _ = q.shape
    assert ctx_len % BLOCK == 0
    body = functools.partial(_attn_body, n_qb=ctx_len // BLOCK, window=False, seq=s, sink_row=j)
    return pl.pallas_call(
        body,
        grid=(1, n_latent),
        in_specs=[pl.BlockSpec(memory_space=pltpu.SMEM),
                  pl.BlockSpec((None, ctx_len, Q_DIM), lambda _, b: (n_latent, b, 0)),
                  pl.BlockSpec((None, KV_DIM, ctx_len), lambda _, b: (n_latent, 0, b)),
                  pl.BlockSpec((None, ctx_len, 2 * KV_DIM), lambda _, b: (n_latent, b, 0))],
        out_specs=pl.BlockSpec((None, ctx_len, Q_DIM), lambda _, b: (0, b, 0)),
        out_shape=jax.ShapeDtypeStruct((1, s, Q_DIM), BF16),
        compiler_params=_params(2),
        name="context_attention",
    )(sink, q, kt, v2)


def _rope_tables(seq):
    rows = seq // GRID_W
    row = jnp.broadcast_to(jnp.arange(rows)[:, None], (rows, GRID_W)).reshape(-1).astype(F32)
    col = jnp.broadcast_to(jnp.arange(GRID_W)[None, :], (rows, GRID_W)).reshape(-1).astype(F32)
    inv_freq = ROPE_THETA ** (-jnp.arange(ROPE_QUARTER, dtype=F32) / ROPE_QUARTER)
    ang_r, ang_c = row[:, None] * inv_freq, col[:, None] * inv_freq
    cos = jnp.concatenate([jnp.cos(ang_r)] * 2 + [jnp.cos(ang_c)] * 2, axis=1)
    sin = jnp.concatenate([-jnp.sin(ang_r), jnp.sin(ang_r), -jnp.sin(ang_c), jnp.sin(ang_c)], axis=1)
    reps = LANES // HEAD_DIM
    cos, sin = jnp.tile(cos, (1, reps)), jnp.tile(sin, (1, reps))
    return jnp.stack([cos, jnp.ones_like(cos)]), jnp.stack([sin, jnp.zeros_like(sin)])


def kernel(x, c, ctx, c_ctx, w_mod, b_mod, norm_g, ffn_w_gu, ffn_w_down, conv_w_in, conv_w, conv_w_out,
           attn_w_qkv, attn_w_o, attn_sink):
    n_latent, seq, d = x.shape
    ctx_len = ctx.shape[1]
    assert d == D_MODEL and n_latent * ctx_len == seq and seq % GRID_W == 0
    assert ctx_len & (ctx_len - 1) == 0 and seq & (seq - 1) == 0
    n_slots = n_latent + 1

    cs = jnp.concatenate([c, c_ctx[None, :], jnp.zeros((MOD_PAD_ROWS - n_slots, d), F32)], axis=0)
    mod_all = _modulation(cs, w_mod, b_mod)[:, :n_slots].reshape(DEPTH, n_slots, N_MOD, d)
    cos, sin = _rope_tables(seq)

    wgu, wd = ffn_w_gu[0, 0].astype(BF16), ffn_w_down[0, 0].astype(BF16)
    xz, ctx_slot = x, ctx.reshape(1, seq, d)
    for i in range(DEPTH):
        last = i == DEPTH - 1
        use_attn = (i % N_MIXERS) == 1
        j = i // N_MIXERS
        live = n_latent if last else n_slots

        mixer_w = ([(attn_w_qkv, (j,)), (attn_w_o, (j,))] if use_attn
                   else [(conv_w_in, (j,)), (conv_w_out, (j,))])
        if i == 0:
            mix_in, mix_out = (w[lead].astype(BF16) for w, lead in mixer_w)
            mixer_w = []
        xz, cast_w = _ffn(
            (xz, ctx_slot), mod_all, norm_g, wgu, wd, [(ffn_w_gu, (i, 1)), (ffn_w_down, (i, 1))] + mixer_w,
            layer=i, half=0, n_slots=n_slots if (use_attn or not last) else n_latent, n_latent=n_latent,
            tm=FFN_ROWS, groups=FFN_ROW_GROUPS)
        wgu, wd = cast_w[:2]
        if mixer_w:
            mix_in, mix_out = cast_w[2:]
        ctx_slot = None
        attn = None
        if use_attn:
            q, kt, v2 = _qkv(xz, mod_all, norm_g, mix_in, cos, sin, layer=i, n_latent=n_latent, tm=MIX_ROWS,
                             q_scale=HEAD_DIM ** -0.5 * LOG2_E)
            o = _attention_latent(q, kt, v2, attn_sink, j=j, n_latent=n_latent, ctx_len=ctx_len, tq=ATTN_ROWS)
            oz = None if last else _attention_context(q, kt, v2, attn_sink, j=j, n_latent=n_latent,
                                                      ctx_len=ctx_len)
            attn = ((o, oz), mix_out)
        else:
            xz = _conv(xz, mod_all, norm_g, mix_in, conv_w, mix_out, layer=i, j=j, n_slots=live,
                       n_latent=n_latent, ctx_len=ctx_len, tm=MIX_ROWS)
        next_ffn = [] if last else [(ffn_w_gu, (i + 1, 0)), (ffn_w_down, (i + 1, 0))]
        xz, next_w = _ffn((xz, None), mod_all, norm_g, wgu, wd, next_ffn, layer=i, half=1, n_slots=live,
                          n_latent=n_latent, tm=FFN_ROWS,
                          groups=FFN_ROW_GROUPS if attn is None else FUSED_FFN_ROW_GROUPS, attn=attn)
        if not last:
            wgu, wd = next_w
    return xz
```

```python
import functools

import jax
import jax.numpy as jnp
from jax import lax
from jax.experimental import pallas as pl
from jax.experimental.pallas import tpu as pltpu

D_MODEL = 1024
DEPTH = 4
GRID_W = 64
N_MIXERS = 2
N_HEADS = 16
N_KV_HEADS = 4
HEAD_DIM = D_MODEL // N_HEADS
GROUP = N_HEADS // N_KV_HEADS
Q_DIM = N_HEADS * HEAD_DIM
KV_DIM = N_KV_HEADS * HEAD_DIM
WINDOW = 128
BLOCK = 128
ROPE_THETA = 10000.0
ROPE_QUARTER = HEAD_DIM // 4
CONV_WIDTH = 3
D_FF = 2816
N_MOD = 9
RMS_EPS = 1e-6
NEG_INF = -1e30
HALF_STEP = 0.5
LOG2_E = 1.4426950408889634

LANES = 128
BF16_SUBLANES = 16
VMEM_LIMIT_BYTES = 56 * 1024 * 1024

FF_CHUNK = 256
CONV_CHUNK = 256
MOD_PAD_ROWS = 16
MOD_COL_BLOCKS = 4
FFN_ROWS = 1024
FFN_ROW_GROUPS = 4
FFN_FINISH_AT = 2
FFN_START_NEXT_AT = 5
MIX_ROWS = 1024
QKV_ROW_GROUPS = 4
ATTN_ROWS = 8 * BLOCK
ATTN_LOOKAHEAD = 1
CAST_STEPS = 16

F32 = jnp.float32
BF16 = jnp.bfloat16

assert 2 * HEAD_DIM == LANES and WINDOW == BLOCK == LANES


def _rms(x, g):
    return x * lax.rsqrt(jnp.mean(x * x, axis=-1, keepdims=True) + RMS_EPS) * g


def _params(n_grid_dims):
    return pltpu.CompilerParams(dimension_semantics=("arbitrary",) * n_grid_dims,
                                vmem_limit_bytes=VMEM_LIMIT_BYTES)


def _resident(arr, *lead):
    rest = arr.shape[len(lead):]
    index = tuple(lead) + (0,) * len(rest)
    return pl.BlockSpec((None,) * len(lead) + rest, lambda *_: index, pipeline_mode=pl.Buffered(1))


def _mod_spec(mod_all, layer):
    return pl.BlockSpec((None, None) + mod_all.shape[2:], lambda b, t: (layer, b, 0, 0))


def _cast_specs(casts, inner_steps):
    def block(a, b):
        return jnp.minimum(a * inner_steps + b, CAST_STEPS - 1)

    cast_in, cast_out, cast_shapes = [], [], []
    for src, lead in casts:
        r, c = src.shape[len(lead):]
        rows = r // CAST_STEPS
        assert r % CAST_STEPS == 0 and rows % BF16_SUBLANES == 0
        cast_in.append(pl.BlockSpec((None,) * len(lead) + (rows, c),
                                    lambda a, b, lead=tuple(lead): lead + (block(a, b), 0)))
        cast_out.append(pl.BlockSpec((rows, c), lambda a, b: (block(a, b), 0)))
        cast_shapes.append(jax.ShapeDtypeStruct((r, c), BF16))
    return cast_in, cast_out, cast_shapes


def _mod_body(c_ref, w_ref, b_ref, *refs):
    n_casts = (len(refs) - 1) // 2
    cast_src, o_ref, cast_dst = refs[:n_casts], refs[n_casts], refs[n_casts + 1:]
    for src, dst in zip(cast_src, cast_dst):
        dst[...] = src[...].astype(BF16)
    c = c_ref[...]
    a = (c * jax.nn.sigmoid(c)).astype(BF16)
    o_ref[...] = jnp.dot(a, w_ref[...].astype(BF16), preferred_element_type=F32) + b_ref[...]


def _modulation(cs, w_mod, b_mod, casts):
    depth, d, n = w_mod.shape
    tn = n // MOD_COL_BLOCKS
    assert depth * MOD_COL_BLOCKS >= CAST_STEPS
    cast_in, cast_out, cast_shapes = _cast_specs(casts, MOD_COL_BLOCKS)
    outs = pl.pallas_call(
        _mod_body,
        grid=(depth, MOD_COL_BLOCKS),
        in_specs=[pl.BlockSpec((MOD_PAD_ROWS, d), lambda l, j: (0, 0)),
                  pl.BlockSpec((None, d, tn), lambda l, j: (l, 0, j)),
                  pl.BlockSpec((None, 1, tn), lambda l, j: (l, 0, j))] + cast_in,
        out_specs=[pl.BlockSpec((None, MOD_PAD_ROWS, tn), lambda l, j: (l, 0, j))] + cast_out,
        out_shape=[jax.ShapeDtypeStruct((depth, MOD_PAD_ROWS, n), F32)] + cast_shapes,
        compiler_params=_params(2),
        name="modulation",
    )(cs, w_mod, b_mod.reshape(depth, 1, n), *[src for src, _ in casts])
    return outs[0], outs[1:]


def _slot_stream(lat, ctx, tm, n_latent):
    width = lat.shape[-1]
    if ctx is None:
        return [pl.BlockSpec((None, tm, width), lambda b, t: (b, t, 0))], [lat]
    return ([pl.BlockSpec((None, tm, width), lambda b, t: (jnp.minimum(b, n_latent - 1), t, 0)),
             pl.BlockSpec((None, tm, width), lambda b, t: (0, jnp.where(b == n_latent, t, 0), 0),
                          pipeline_mode=pl.Buffered(1))], [lat, ctx])


def _ffn_body(*refs, k, gi, groups, n_latent, n_x, n_a, n_casts):
    refs = list(refs)
    x_refs, refs = refs[:n_x], refs[n_x:]
    a_refs, refs = refs[:n_a], refs[n_a:]
    wo_ref = refs.pop(0) if n_a else None
    mod_ref, g_ref, wgu_ref, wd_ref = refs[:4]
    cast_src, refs = refs[4:4 + n_casts], refs[4 + n_casts:]
    o_ref = refs.pop(0)
    cast_dst, (h_ref, act_ref) = refs[:n_casts], refs[n_casts:]
    for src, dst in zip(cast_src, cast_dst):
        dst[...] = src[...].astype(BF16)
    in_ctx_slot = pl.program_id(0) == n_latent

    def pick(stream, rs):
        lat = stream[0][rs, :]
        return lat if len(stream) == 1 else jnp.where(in_ctx_slot, stream[1][rs, :], lat)

    shift, scale, gate = mod_ref[k:k + 1, :], mod_ref[k + 1:k + 2, :], mod_ref[k + 2:k + 3, :]
    rows = o_ref.shape[0] // groups

    def finish(rs, x):
        y = jnp.dot(act_ref[rs, :], wd_ref[...], preferred_element_type=F32)
        o_ref[rs, :] = x + HALF_STEP * gate * _rms(y, g_ref[gi + 1:gi + 2, :])

    def start(r):
        rs = slice(r * rows, (r + 1) * rows)
        x = pick(x_refs, rs)
        if n_a:
            mixed = jnp.dot(pick(a_refs, rs), wo_ref[...], preferred_element_type=F32)
            x = x + mod_ref[5:6, :] * _rms(mixed, g_ref[3:4, :])
        h_ref[rs, :] = (_rms(x, g_ref[gi:gi + 1, :]) * (1.0 + scale) + shift).astype(BF16)
        return rs, x

    skew_a, skew_b = FFN_FINISH_AT, FFN_START_NEXT_AT
    pending = None
    cur = start(0)
    for r in range(groups):
        rs, x = cur
        nxt = None
        for c in range(D_FF // FF_CHUNK):
            cs = slice(c * FF_CHUNK, (c + 1) * FF_CHUNK)
            us = slice(D_FF + c * FF_CHUNK, D_FF + (c + 1) * FF_CHUNK)
            gt = jnp.dot(h_ref[rs, :], wgu_ref[:, cs], preferred_element_type=F32)
            up = jnp.dot(h_ref[rs, :], wgu_ref[:, us], preferred_element_type=F32)
            act_ref[rs, cs] = (gt * jax.nn.sigmoid(gt) * up).astype(BF16)
            if c == skew_a and pending is not None:
                finish(*pending)
            if c == skew_b and r + 1 < groups:
                nxt = start(r + 1)
        pending = cur
        if r + 1 < groups:
            cur = nxt if nxt is not None else start(r + 1)
    finish(*pending)


def _ffn(x, mod_all, norm_g, wgu, wd, casts, *, layer, half, n_slots, n_latent, tm, groups, attn=None):
    _, s, d = x[0].shape
    steps_per_slot = s // tm
    assert n_slots * steps_per_slot >= CAST_STEPS
    x_specs, x_ops = _slot_stream(*x, tm, n_latent)
    a_specs, a_ops = [], []
    if attn is not None:
        a_specs, a_ops = _slot_stream(*attn[0], tm, n_latent)
        a_specs, a_ops = a_specs + [_resident(attn[1])], a_ops + [attn[1]]
    body = functools.partial(_ffn_body, k=6 * half, gi=4 * half, groups=groups, n_latent=n_latent,
                             n_x=len(x_ops), n_a=max(len(a_ops) - 1, 0), n_casts=len(casts))

    cast_in, cast_out, cast_shapes = _cast_specs(casts, steps_per_slot)
    outs = pl.pallas_call(
        body,
        grid=(n_slots, steps_per_slot),
        in_specs=x_specs + a_specs + [_mod_spec(mod_all, layer), _resident(norm_g, layer), _resident(wgu),
                                      _resident(wd)] + cast_in,
        out_specs=[pl.BlockSpec((None, tm, d), lambda b, t: (b, t, 0))] + cast_out,
        out_shape=[jax.ShapeDtypeStruct((n_slots, s, d), F32)] + cast_shapes,
        scratch_shapes=[pltpu.VMEM((tm, d), BF16), pltpu.VMEM((tm, D_FF), BF16)],
        compiler_params=_params(2),
        name="ffn",
    )(*x_ops, *a_ops, mod_all, norm_g, wgu, wd, *[src for src, _ in casts])
    return outs[0], outs[1:]


def _conv_body(x_ref, xp_ref, xn_ref, mod_ref, g_ref, win_ref, cw_ref, wout_ref, o_ref,
               h_ref, gy_ref, *, tm, n_latent, seq, ctx_len):
    halo = BF16_SUBLANES
    rows = tm + 2 * halo
    slot, t = pl.program_id(0), pl.program_id(1)
    shift, scale, gate = mod_ref[3:4, :], mod_ref[4:5, :], mod_ref[5:6, :]
    g_in = g_ref[2:3, :]

    def modulated(v):
        return (_rms(v, g_in) * (1.0 + scale) + shift).astype(BF16)

    x = x_ref[...]
    h_ref[0:tm, :] = modulated(x)
    h_ref[tm:tm + halo, :] = modulated(xn_ref[...])
    h_ref[tm + halo:rows, :] = modulated(xp_ref[...])

    period = jnp.where(slot == n_latent, ctx_len, seq)
    pos = (t * tm + lax.broadcasted_iota(jnp.int32, (tm, 1), 0)) & (period - 1)
    has_prev = pos != 0
    has_next = pos != period - 1

    def project(c):
        cols = [slice(part * D_MODEL + c * CONV_CHUNK, part * D_MODEL + (c + 1) * CONV_CHUNK) for part in range(3)]
        return (jnp.dot(h_ref[0:tm, :], win_ref[:, cols[0]], preferred_element_type=F32),
                jnp.dot(h_ref[...], win_ref[:, cols[1]], preferred_element_type=F32),
                jnp.dot(h_ref[...], win_ref[:, cols[2]], preferred_element_type=F32))

    def mix(c, bg, cg, u):
        cs = slice(c * CONV_CHUNK, (c + 1) * CONV_CHUNK)
        cu = cg * u
        prev = jnp.where(has_prev, pltpu.roll(cu, 1, 0)[0:tm, :], 0.0)
        nxt = jnp.where(has_next, pltpu.roll(cu, rows - 1, 0)[0:tm, :], 0.0)
        y = cw_ref[0:1, cs] * prev + cw_ref[1:2, cs] * cu[0:tm, :] + cw_ref[2:3, cs] * nxt
        gy_ref[:, cs] = (bg * y).astype(BF16)

    n_chunks = D_MODEL // CONV_CHUNK
    ready = project(0)
    for c in range(n_chunks):
        ahead = project(c + 1) if c + 1 < n_chunks else None
        mix(c, *ready)
        ready = ahead
    y = jnp.dot(gy_ref[...], wout_ref[...], preferred_element_type=F32)
    o_ref[...] = x + gate * _rms(y, g_ref[3:4, :])


def _conv(xz, mod_all, norm_g, win, cw, wout, *, layer, j, n_slots, n_latent, ctx_len, tm):
    _, s, d = xz.shape
    halo = BF16_SUBLANES
    hb = tm // halo
    last = s // halo - 1
    body = functools.partial(_conv_body, tm=tm, n_latent=n_latent, seq=s, ctx_len=ctx_len)
    return pl.pallas_call(
        body,
        grid=(n_slots, s // tm),
        in_specs=[pl.BlockSpec((None, tm, d), lambda b, t: (b, t, 0)),
                  pl.BlockSpec((None, halo, d), lambda b, t: (b, jnp.maximum(t * hb - 1, 0), 0)),
                  pl.BlockSpec((None, halo, d), lambda b, t: (b, jnp.minimum((t + 1) * hb, last), 0)),
                  _mod_spec(mod_all, layer), _resident(norm_g, layer), _resident(win),
                  _resident(cw, j), _resident(wout)],
        out_specs=pl.BlockSpec((None, tm, d), lambda b, t: (b, t, 0)),
        out_shape=jax.ShapeDtypeStruct((n_slots, s, d), F32),
        scratch_shapes=[pltpu.VMEM((tm + 2 * halo, d), BF16), pltpu.VMEM((tm, d), BF16)],
        compiler_params=_params(2),
        name="conv_mixer",
    )(xz, xz, xz, mod_all, norm_g, win, cw, wout)


def _qkv_body(x_ref, mod_ref, g_ref, w_ref, cos_ref, sin_ref, q_ref, kt_ref, v_ref, h_ref, *, q_scale):
    shift, scale = mod_ref[3:4, :], mod_ref[4:5, :]
    first_half = (lax.broadcasted_iota(jnp.int32, (1, LANES), 1) & (2 * ROPE_QUARTER - 1)) < ROPE_QUARTER
    grp = x_ref.shape[0] // QKV_ROW_GROUPS
    groups = [slice(r * grp, (r + 1) * grp) for r in range(QKV_ROW_GROUPS)]

    def project(rs):
        h_ref[rs, :] = (_rms(x_ref[rs, :], g_ref[2:3, :]) * (1.0 + scale) + shift).astype(BF16)
        return (jnp.dot(h_ref[rs, :], w_ref[:, :Q_DIM + KV_DIM], preferred_element_type=F32),
                jnp.dot(h_ref[rs, :], w_ref[:, Q_DIM + KV_DIM:], preferred_element_type=F32))

    def emit(rs, qk, v):
        cos, sin = cos_ref[rs, :], sin_ref[rs, :]

        def rope(t):
            partner = jnp.where(first_half, pltpu.roll(t, LANES - ROPE_QUARTER, 1),
                                pltpu.roll(t, ROPE_QUARTER, 1))
            return t * cos + partner * sin

        for j in range(Q_DIM // LANES):
            q_ref[rs, j * LANES:(j + 1) * LANES] = (
                rope(qk[:, j * LANES:(j + 1) * LANES]) * q_scale).astype(BF16)
        for j in range(KV_DIM // LANES):
            kt_ref[j * LANES:(j + 1) * LANES, rs] = rope(
                qk[:, Q_DIM + j * LANES:Q_DIM + (j + 1) * LANES]).T.astype(BF16)
        heads = [v[:, h * HEAD_DIM:(h + 1) * HEAD_DIM] for h in range(N_KV_HEADS)]
        v_ref[rs, :] = jnp.concatenate([vh for vh in heads for _ in range(2)], axis=1).astype(BF16)

    ready = project(groups[0])
    for r, rs in enumerate(groups):
        ahead = project(groups[r + 1]) if r + 1 < len(groups) else None
        emit(rs, *ready)
        ready = ahead


def _qkv(xz, mod_all, norm_g, w, cos, sin, *, layer, n_latent, tm, q_scale):
    n_slots, s, d = xz.shape
    tok = lambda b, t: (b, t, 0)
    tab = lambda b, t: (jnp.where(b == n_latent, 1, 0), t, 0)
    return pl.pallas_call(
        functools.partial(_qkv_body, q_scale=q_scale),
        grid=(n_slots, s // tm),
        in_specs=[pl.BlockSpec((None, tm, d), tok),
                  _mod_spec(mod_all, layer), _resident(norm_g, layer), _resident(w),
                  pl.BlockSpec((None, tm, LANES), tab), pl.BlockSpec((None, tm, LANES), tab)],
        out_specs=[pl.BlockSpec((None, tm, Q_DIM), tok),
                   pl.BlockSpec((None, KV_DIM, tm), lambda b, t: (b, 0, t)),
                   pl.BlockSpec((None, tm, 2 * KV_DIM), tok)],
        out_shape=[jax.ShapeDtypeStruct((n_slots, s, Q_DIM), BF16),
                   jax.ShapeDtypeStruct((n_slots, KV_DIM, s), BF16),
                   jax.ShapeDtypeStruct((n_slots, s, 2 * KV_DIM), BF16)],
        scratch_shapes=[pltpu.VMEM((tm, d), BF16)],
        compiler_params=_params(2),
        name="qkv_rope",
    )(xz, mod_all, norm_g, w, cos, sin)


def _attn_body(sink_ref, q_ref, *refs, n_qb, window, seq, sink_row):
    if window:
        ktp_ref, ktc_ref, ktn_ref, ktz_ref, vp_ref, vc_ref, vn_ref, vz_ref, o_ref = refs
        kt_all = jnp.concatenate([ktp_ref[...], ktc_ref[...], ktn_ref[...]], axis=1)
        v_all = jnp.concatenate([vp_ref[...], vc_ref[...], vn_ref[...]], axis=0)
    else:
        ktz_ref, vz_ref, o_ref = refs
    t = pl.program_id(1)
    row = lax.broadcasted_iota(jnp.int32, (BLOCK, BLOCK), 0)
    col = lax.broadcasted_iota(jnp.int32, (BLOCK, BLOCK), 1)
    low_lanes = lax.broadcasted_iota(jnp.int32, (1, LANES), 1) < HEAD_DIM
    def operands(i, h):
        hr = slice(h * HEAD_DIM, (h + 1) * HEAD_DIM)
        hl = slice(h * LANES, (h + 1) * LANES)
        if window:
            keys = jnp.concatenate([kt_all[hr, i * BLOCK:(i + 3) * BLOCK], ktz_ref[hr, :]], axis=1)
            vals = jnp.concatenate([v_all[i * BLOCK:(i + 3) * BLOCK, hl], vz_ref[:, hl]], axis=0)
        else:
            keys, vals = ktz_ref[hr, :], vz_ref[:, hl]
        zk = jnp.zeros_like(keys)
        rhs = jnp.concatenate([jnp.concatenate([keys, zk], axis=1),
                               jnp.concatenate([zk, keys], axis=1)], axis=0)
        low_v = lax.broadcasted_iota(jnp.int32, vals.shape, 1) < HEAD_DIM
        zv = jnp.zeros_like(vals)
        ones_lo = jnp.where(low_v, 1.0, 0.0).astype(BF16)
        ones_hi = jnp.where(low_v, 0.0, 1.0).astype(BF16)
        vv = jnp.concatenate([jnp.concatenate([jnp.where(low_v, vals, zv), ones_lo], axis=1),
                              jnp.concatenate([jnp.where(low_v, zv, vals), ones_hi], axis=1)], axis=0)
        return rhs, vv

    def scores(i, j, rhs):
        return jnp.dot(q_ref[i * BLOCK:(i + 1) * BLOCK, j * LANES:(j + 1) * LANES], rhs,
                       preferred_element_type=F32)

    def finish(i, j, s, vv):
        nk = s.shape[1] // 2
        if window:
            n = t * n_qb + i
            keep_prev = (col >= row) & (n > 0)
            keep_next = (col <= row) & (n < seq // BLOCK - 1)
        ps, sink_terms = [], []
        for e in range(2):
            sink = sink_ref[sink_row, 2 * j + e] * LOG2_E
            se = s[:, e * nk:(e + 1) * nk]
            if window:
                se = jnp.concatenate(
                    [jnp.where(keep_prev, se[:, :BLOCK], NEG_INF), se[:, BLOCK:2 * BLOCK],
                     jnp.where(keep_next, se[:, 2 * BLOCK:3 * BLOCK], NEG_INF), se[:, 3 * BLOCK:]], axis=1)
            m = jnp.maximum(jnp.max(se, axis=-1, keepdims=True), sink)
            ps.append(jnp.exp2(se - m).astype(BF16))
            sink_terms.append(jnp.exp2(sink - m))
        ov = jnp.dot(jnp.concatenate(ps, axis=1), vv, preferred_element_type=F32)
        denom = ov[:, LANES:] + jnp.where(low_lanes, sink_terms[0], sink_terms[1])
        o_ref[i * BLOCK:(i + 1) * BLOCK, j * LANES:(j + 1) * LANES] = (ov[:, :LANES] / denom).astype(BF16)

    in_flight = []
    for i in range(n_qb):
        for h in range(N_KV_HEADS):
            rhs, vv = operands(i, h)
            for jj in range(GROUP // 2):
                j = (GROUP // 2) * h + jj
                in_flight.append((i, j, scores(i, j, rhs), vv))
                if len(in_flight) > ATTN_LOOKAHEAD:
                    finish(*in_flight.pop(0))
    for item in in_flight:
        finish(*item)


def _attention_latent(q, kt, v2, sink, *, j, n_latent, ctx_len, tq):
    _, s, _ = q.shape
    assert tq % BLOCK == 0 and s % tq == 0
    qpb = tq // BLOCK
    nb = s // BLOCK
    body = functools.partial(_attn_body, n_qb=qpb, window=True, seq=s, sink_row=j)
    kt_spec = lambda w, f: pl.BlockSpec((None, KV_DIM, w), f)
    v_spec = lambda w, f: pl.BlockSpec((None, w, 2 * KV_DIM), f)
    return pl.pallas_call(
        body,
        grid=(n_latent, s // tq),
        in_specs=[pl.BlockSpec(memory_space=pltpu.SMEM),
                  pl.BlockSpec((None, tq, Q_DIM), lambda b, t: (b, t, 0)),
                  kt_spec(BLOCK, lambda b, t: (b, 0, jnp.maximum(t * qpb - 1, 0))),
                  kt_spec(tq, lambda b, t: (b, 0, t)),
                  kt_spec(BLOCK, lambda b, t: (b, 0, jnp.minimum((t + 1) * qpb, nb - 1))),
                  kt_spec(ctx_len, lambda b, t: (n_latent, 0, b)),
                  v_spec(BLOCK, lambda b, t: (b, jnp.maximum(t * qpb - 1, 0), 0)),
                  v_spec(tq, lambda b, t: (b, t, 0)),
                  v_spec(BLOCK, lambda b, t: (b, jnp.minimum((t + 1) * qpb, nb - 1), 0)),
                  v_spec(ctx_len, lambda b, t: (n_latent, b, 0))],
        out_specs=pl.BlockSpec((None, tq, Q_DIM), lambda b, t: (b, t, 0)),
        out_shape=jax.ShapeDtypeStruct((n_latent, s, Q_DIM), BF16),
        compiler_params=_params(2),
        name="window_attention",
    )(sink, q, kt, kt, kt, kt, v2, v2, v2, v2)


def _attention_context(q, kt, v2, sink, *, j, n_latent, ctx_len):
    _, s, _ = q.shape
    assert ctx_len % BLOCK == 0
    body = functools.partial(_attn_body, n_qb=ctx_len // BLOCK, window=False, seq=s, sink_row=j)
    return pl.pallas_call(
        body,
        grid=(1, n_latent),
        in_specs=[pl.BlockSpec(memory_space=pltpu.SMEM),
                  pl.BlockSpec((None, ctx_len, Q_DIM), lambda _, b: (n_latent, b, 0)),
                  pl.BlockSpec((None, KV_DIM, ctx_len), lambda _, b: (n_latent, 0, b)),
                  pl.BlockSpec((None, ctx_len, 2 * KV_DIM), lambda _, b: (n_latent, b, 0))],
        out_specs=pl.BlockSpec((None, ctx_len, Q_DIM), lambda _, b: (0, b, 0)),
        out_shape=jax.ShapeDtypeStruct((1, s, Q_DIM), BF16),
        compiler_params=_params(2),
        name="context_attention",
    )(sink, q, kt, v2)


def _rope_tables(seq):
    rows = seq // GRID_W
    row = jnp.broadcast_to(jnp.arange(rows)[:, None], (rows, GRID_W)).reshape(-1).astype(F32)
    col = jnp.broadcast_to(jnp.arange(GRID_W)[None, :], (rows, GRID_W)).reshape(-1).astype(F32)
    inv_freq = ROPE_THETA ** (-jnp.arange(ROPE_QUARTER, dtype=F32) / ROPE_QUARTER)
    ang_r, ang_c = row[:, None] * inv_freq, col[:, None] * inv_freq
    cos = jnp.concatenate([jnp.cos(ang_r)] * 2 + [jnp.cos(ang_c)] * 2, axis=1)
    sin = jnp.concatenate([-jnp.sin(ang_r), jnp.sin(ang_r), -jnp.sin(ang_c), jnp.sin(ang_c)], axis=1)
    reps = LANES // HEAD_DIM
    cos, sin = jnp.tile(cos, (1, reps)), jnp.tile(sin, (1, reps))
    return jnp.stack([cos, jnp.ones_like(cos)]), jnp.stack([sin, jnp.zeros_like(sin)])


def kernel(x, c, ctx, c_ctx, w_mod, b_mod, norm_g, ffn_w_gu, ffn_w_down, conv_w_in, conv_w, conv_w_out,
           attn_w_qkv, attn_w_o, attn_sink):
    n_latent, seq, d = x.shape
    ctx_len = ctx.shape[1]
    assert d == D_MODEL and n_latent * ctx_len == seq and seq % GRID_W == 0
    assert ctx_len & (ctx_len - 1) == 0 and seq & (seq - 1) == 0
    n_slots = n_latent + 1

    cs = jnp.concatenate([c, c_ctx[None, :], jnp.zeros((MOD_PAD_ROWS - n_slots, d), F32)], axis=0)
    assert N_MIXERS > 1
    mod_all, (wgu, wd, mix_in, mix_out) = _modulation(
        cs, w_mod, b_mod, [(ffn_w_gu, (0, 0)), (ffn_w_down, (0, 0)), (conv_w_in, (0,)), (conv_w_out, (0,))])
    mod_all = mod_all[:, :n_slots].reshape(DEPTH, n_slots, N_MOD, d)
    cos, sin = _rope_tables(seq)
    xz, ctx_slot = x, ctx.reshape(1, seq, d)
    for i in range(DEPTH):
        last = i == DEPTH - 1
        use_attn = (i % N_MIXERS) == 1
        j = i // N_MIXERS
        live = n_latent if last else n_slots

        mixer_w = ([(attn_w_qkv, (j,)), (attn_w_o, (j,))] if use_attn
                   else [(conv_w_in, (j,)), (conv_w_out, (j,))])
        if i == 0:
            mixer_w = []
        xz, cast_w = _ffn(
            (xz, ctx_slot), mod_all, norm_g, wgu, wd, [(ffn_w_gu, (i, 1)), (ffn_w_down, (i, 1))] + mixer_w,
            layer=i, half=0, n_slots=n_slots if (use_attn or not last) else n_latent, n_latent=n_latent,
            tm=FFN_ROWS, groups=FFN_ROW_GROUPS)
        wgu, wd = cast_w[:2]
        if mixer_w:
            mix_in, mix_out = cast_w[2:]
        ctx_slot = None
        attn = None
        if use_attn:
            q, kt, v2 = _qkv(xz, mod_all, norm_g, mix_in, cos, sin, layer=i, n_latent=n_latent, tm=MIX_ROWS,
                             q_scale=HEAD_DIM ** -0.5 * LOG2_E)
            o = _attention_latent(q, kt, v2, attn_sink, j=j, n_latent=n_latent, ctx_len=ctx_len, tq=ATTN_ROWS)
            oz = None if last else _attention_context(q, kt, v2, attn_sink, j=j, n_latent=n_latent,
                                                      ctx_len=ctx_len)
            attn = ((o, oz), mix_out)
        else:
            xz = _conv(xz, mod_all, norm_g, mix_in, conv_w, mix_out, layer=i, j=j, n_slots=live,
                       n_latent=n_latent, ctx_len=ctx_len, tm=MIX_ROWS)
        next_ffn = [] if last else [(ffn_w_gu, (i + 1, 0)), (ffn_w_down, (i + 1, 0))]
        xz, next_w = _ffn((xz, None), mod_all, norm_g, wgu, wd, next_ffn, layer=i, half=1, n_slots=live,
                          n_latent=n_latent, tm=FFN_ROWS, groups=FFN_ROW_GROUPS, attn=attn)
        if not last:
            wgu, wd = next_w
    return xz
```
